```python
import jax, jax.numpy as jnp
from jax import lax
import numpy as np

D_MODEL = 1024
BATCH = 2
SEQ = 8192
DEPTH = 2

HEAD_DIM = 64
POOL_WINDOWS = (2, 4, 8, 16)
POOL_GROUPS = 4
POOL_GROUP_DIM = D_MODEL // 16
POOL_WIDTH = POOL_GROUPS * POOL_GROUP_DIM
N_Q_HEADS = D_MODEL // 128
N_KV_HEADS = 2
Q_PER_KV = N_Q_HEADS // N_KV_HEADS
WINDOW = 128
ATTN_BLOCK = 128
ATTN_WIDTH = N_Q_HEADS * HEAD_DIM
KV_WIDTH = N_KV_HEADS * HEAD_DIM
CHUNK = 128
SGU_GROUPS = 4
SGU_GROUP_DIM = D_MODEL // 16
SGU_WIDTH = SGU_GROUPS * SGU_GROUP_DIM
N_BRANCHES = 3
IN_COLS = POOL_WIDTH + ATTN_WIDTH + 2 * KV_WIDTH + 2 * SGU_WIDTH + N_BRANCHES * D_MODEL
D_FF = 2816
CONV_WIDTH = 3
ROPE_THETA = 10000.0
EPS = 1e-6

kernel_name = "hybrid_pool_swa_sgu_convffn"


def rms_norm(x, g):
    xf = x.astype(jnp.float32)
    y = xf * lax.rsqrt(jnp.mean(xf * xf, axis=-1, keepdims=True) + EPS)
    return (y * g.astype(jnp.float32)).astype(x.dtype)


def rope_tables(positions):
    inv_freq = ROPE_THETA ** (-jnp.arange(0, HEAD_DIM, 2, dtype=jnp.float32) / HEAD_DIM)
    ang = positions.astype(jnp.float32)[..., None] * inv_freq
    return jnp.cos(ang)[:, :, None, :], jnp.sin(ang)[:, :, None, :]


def apply_rope(t, cos, sin):
    tf = t.astype(jnp.float32)
    t1, t2 = jnp.split(tf, 2, axis=-1)
    return jnp.concatenate([t1 * cos - t2 * sin, t2 * cos + t1 * sin], axis=-1).astype(t.dtype)


def pool_mixer(xa, w_pool, pool_scale):
    B, S, _ = xa.shape
    xf = xa.astype(jnp.float32)
    cs = jnp.concatenate([jnp.zeros((B, 1, POOL_WIDTH), jnp.float32), jnp.cumsum(xf, axis=1)], axis=1)
    t = jnp.arange(S)
    pooled = []
    for g, w in enumerate(POOL_WINDOWS):
        c = cs[..., g * POOL_GROUP_DIM:(g + 1) * POOL_GROUP_DIM]
        upper = c[:, 1:]
        lower = jnp.concatenate([jnp.zeros((B, w - 1, POOL_GROUP_DIM), jnp.float32), c[:, :S - w + 1]], axis=1)
        count = jnp.minimum(t + 1, w).astype(jnp.float32)[None, :, None]
        pooled.append((upper - lower) / count)
    pooled = jnp.stack(pooled, axis=2)
    diff = (pooled - xf.reshape(B, S, POOL_GROUPS, POOL_GROUP_DIM)).astype(xa.dtype)
    mixed = jnp.einsum('bsgc,gcd->bsgd', diff, w_pool).reshape(B, S, POOL_WIDTH)
    return mixed * pool_scale


def swa_attention(q, k, v, sinks):
    B, S = q.shape[:2]
    nb = S // ATTN_BLOCK
    qb = q.reshape(B, nb, ATTN_BLOCK, N_KV_HEADS, Q_PER_KV, HEAD_DIM)

    def band(t):
        tb = t.reshape(B, nb, ATTN_BLOCK, N_KV_HEADS, HEAD_DIM)
        prev = jnp.concatenate([jnp.zeros_like(tb[:, :1]), tb[:, :-1]], axis=1)
        return jnp.concatenate([prev, tb], axis=2)

    kb, vb = band(k), band(v)
    scores = jnp.einsum('bnqhgd,bnkhd->bnhgqk', qb, kb).astype(jnp.float32) * (HEAD_DIM ** -0.5)
    qi = jnp.arange(ATTN_BLOCK)[:, None]
    kj = jnp.arange(2 * ATTN_BLOCK)[None, :]
    dist = qi + ATTN_BLOCK - kj
    in_window = (dist >= 0) & (dist < WINDOW)
    key_pos = (jnp.arange(nb)[:, None, None] - 1) * ATTN_BLOCK + kj[None]
    mask = in_window[None] & (key_pos >= 0)
    scores = jnp.where(mask[None, :, None, None], scores, -jnp.inf)
    sink = sinks.astype(jnp.float32).reshape(N_KV_HEADS, Q_PER_KV)[None, None, :, :, None, None]
    sink = jnp.broadcast_to(sink, scores.shape[:-1] + (1,))
    probs = jax.nn.softmax(jnp.concatenate([scores, sink], axis=-1), axis=-1)[..., :-1]
    out = jnp.einsum('bnhgqk,bnkhd->bnqhgd', probs.astype(v.dtype), vb)
    return out.reshape(B, S, ATTN_WIDTH)


def spatial_gating(u, v, w_s, b_s, v_norm):
    B, S, _ = u.shape
    nc = S // CHUNK
    u = jax.nn.gelu(u)
    vg = rms_norm(jax.nn.gelu(v).reshape(B, S, SGU_GROUPS, SGU_GROUP_DIM), v_norm)
    vc = vg.reshape(B, nc, CHUNK, SGU_GROUPS, SGU_GROUP_DIM)
    w_causal = jnp.tril(w_s)
    s = jnp.einsum('gts,bnsgc->bntgc', w_causal, vc) + b_s.T[None, None, :, :, None]
    return u * s.reshape(B, S, SGU_WIDTH)


def causal_dwconv(x, w, b):
    C = x.shape[-1]
    y = lax.conv_general_dilated(
        x, w[:, None, :].astype(x.dtype), window_strides=(1,),
        padding=((CONV_WIDTH - 1, 0),), dimension_numbers=('NWC', 'WIO', 'NWC'),
        feature_group_count=C)
    return y + b


def setup_inputs(seed: int = 0) -> dict:
    key = jax.random.key(seed)
    ks = jax.random.split(key, 24)
    f32 = jnp.float32
    nrm = lambda k, shape, s: jax.random.normal(k, shape, f32) * s
    return {
        "x": nrm(ks[0], (BATCH, SEQ, D_MODEL), 1.0),
        "positions": (jnp.arange(SEQ, dtype=jnp.int32)[None, :]
                      + jax.random.randint(ks[1], (BATCH, 1), 0, SEQ, dtype=jnp.int32)),
        "norm1": 1.0 + nrm(ks[2], (DEPTH, D_MODEL), 0.02),
        "w_in": nrm(ks[3], (DEPTH, D_MODEL, IN_COLS), D_MODEL ** -0.5),
        "q_norm": 1.0 + nrm(ks[4], (DEPTH, HEAD_DIM), 0.02),
        "k_norm": 1.0 + nrm(ks[5], (DEPTH, HEAD_DIM), 0.02),
        "sinks": nrm(ks[6], (DEPTH, N_Q_HEADS), 0.5),
        "w_pool": nrm(ks[7], (DEPTH, POOL_GROUPS, POOL_GROUP_DIM, POOL_GROUP_DIM), POOL_GROUP_DIM ** -0.5),
        "pool_scale": 1.0 + nrm(ks[8], (DEPTH, POOL_WIDTH), 0.02),
        "sgu_v_norm": 1.0 + nrm(ks[9], (DEPTH, SGU_GROUP_DIM), 0.02),
        "w_s": nrm(ks[10], (DEPTH, SGU_GROUPS, CHUNK, CHUNK), CHUNK ** -0.5),
        "b_s": 1.0 + nrm(ks[11], (DEPTH, SGU_GROUPS, CHUNK), 0.02),
        "w_proj_a": nrm(ks[12], (DEPTH, POOL_WIDTH, D_MODEL), POOL_WIDTH ** -0.5),
        "w_proj_b": nrm(ks[13], (DEPTH, ATTN_WIDTH, D_MODEL), ATTN_WIDTH ** -0.5),
        "w_proj_c": nrm(ks[14], (DEPTH, SGU_WIDTH, D_MODEL), SGU_WIDTH ** -0.5),
        "w_out": nrm(ks[15], (DEPTH, D_MODEL, D_MODEL), D_MODEL ** -0.5),
        "norm2": 1.0 + nrm(ks[16], (DEPTH, D_MODEL), 0.02),
        "w_up": nrm(ks[17], (DEPTH, D_MODEL, 2 * D_FF), D_MODEL ** -0.5),
        "conv_w": nrm(ks[18], (DEPTH, CONV_WIDTH, 2 * D_FF), CONV_WIDTH ** -0.5),
        "conv_b": nrm(ks[19], (DEPTH, 2 * D_FF), 0.02),
        "w_down": nrm(ks[20], (DEPTH, D_FF, D_MODEL), D_FF ** -0.5),
    }


def reference(x, positions, norm1, w_in, q_norm, k_norm, sinks, w_pool, pool_scale,
              sgu_v_norm, w_s, b_s, w_proj_a, w_proj_b, w_proj_c, w_out, norm2,
              w_up, conv_w, conv_b, w_down):
    B, S, _ = x.shape
    cos, sin = rope_tables(positions)
    splits = np.cumsum([POOL_WIDTH, ATTN_WIDTH, KV_WIDTH, KV_WIDTH, SGU_WIDTH, SGU_WIDTH]).tolist()
    for l in range(DEPTH):
        h = rms_norm(x, norm1[l])
        z = h @ w_in[l]
        x_pool, q, k, v, u_s, v_s, gates = jnp.split(z, splits, axis=-1)

        y_a = pool_mixer(x_pool, w_pool[l], pool_scale[l]) @ w_proj_a[l]

        q = apply_rope(rms_norm(q.reshape(B, S, N_Q_HEADS, HEAD_DIM), q_norm[l]), cos, sin)
        k = apply_rope(rms_norm(k.reshape(B, S, N_KV_HEADS, HEAD_DIM), k_norm[l]), cos, sin)
        v = v.reshape(B, S, N_KV_HEADS, HEAD_DIM)
        y_b = swa_attention(q, k, v, sinks[l]) @ w_proj_b[l]

        y_c = spatial_gating(u_s, v_s, w_s[l], b_s[l], sgu_v_norm[l]) @ w_proj_c[l]

        g = jax.nn.sigmoid(gates.astype(jnp.float32)).astype(x.dtype).reshape(B, S, N_BRANCHES, D_MODEL)
        merged = g[:, :, 0] * y_a + g[:, :, 1] * y_b + g[:, :, 2] * y_c
        x = x + merged @ w_out[l]

        h = rms_norm(x, norm2[l])
        up = causal_dwconv(h @ w_up[l], conv_w[l], conv_b[l])
        gate, val = jnp.split(up, 2, axis=-1)
        x = x + (jax.nn.silu(gate) * val) @ w_down[l]
    return x
```

```python
import functools

import jax
import jax.numpy as jnp
from jax import lax
from jax.experimental import pallas as pl
from jax.experimental.pallas import tpu as pltpu

D_MODEL = 1024
DEPTH = 2
HEAD_DIM = 64
POOL_WINDOWS = (2, 4, 8, 16)
POOL_GROUP_DIM = 64
POOL_WIDTH = 256
N_Q_HEADS = 8
N_KV_HEADS = 2
Q_PER_KV = 4
ATTN_BLOCK = 128
ATTN_WIDTH = 512
KV_WIDTH = 128
CHUNK = 128
SGU_WIDTH = 256
N_BRANCHES = 3
D_FF = 2816
CONV_WIDTH = 3
ROPE_THETA = 10000.0
EPS = 1e-6

_C_POOL = 0
_C_Q = _C_POOL + POOL_WIDTH
_C_K = _C_Q + ATTN_WIDTH
_C_V = _C_K + KV_WIDTH
_C_SU = _C_V + KV_WIDTH
_C_SV = _C_SU + SGU_WIDTH
_C_GATE = _C_SV + SGU_WIDTH
IN_COLS = _C_GATE + N_BRANCHES * D_MODEL

LANES = 128
SUBLANES = 8
POOL_HALO = 16
CONV_HALO = SUBLANES
MIXER_TILE = 256
FFN_TILE = 256
FFN_CHUNK = 256
ROPE_TILE = 2048
VMEM_LIMIT = 56 * 1024 * 1024

F32 = jnp.float32
BF16 = jnp.bfloat16


def _dot(a, b):
    return jnp.dot(a, b, preferred_element_type=F32)


def _const_spec(shape):
    zeros = (0,) * len(shape)
    return pl.BlockSpec(shape, lambda *_: zeros, pipeline_mode=pl.Buffered(1))


def _rms_rows(x, gain):
    ms = jnp.mean(x * x, axis=-1, keepdims=True)
    return x * lax.rsqrt(ms + EPS) * gain


def _segsum(v, seg):
    hi = v.astype(BF16)
    lo = (v - hi.astype(F32)).astype(BF16)
    return _dot(hi, seg) + _dot(lo, seg)


def _swap_halves(x):
    lane = lax.broadcasted_iota(jnp.int32, x.shape, 1)
    fwd = pltpu.roll(x, LANES - HEAD_DIM // 2, axis=1)
    bwd = pltpu.roll(x, HEAD_DIM // 2, axis=1)
    return jnp.where((lane & (HEAD_DIM // 2)) == 0, fwd, bwd)


def _dup_heads(x):
    lane = lax.broadcasted_iota(jnp.int32, x.shape, 1)
    rolled = pltpu.roll(x, HEAD_DIM, axis=1)
    lo = lane < HEAD_DIM
    return jnp.where(lo, x, rolled), jnp.where(lo, rolled, x)


def _rope_kernel(pos_ref, freq_ref, cos_ref, sin_ref):
    ang = pos_ref[0].astype(F32) * freq_ref[...]
    lane = lax.broadcasted_iota(jnp.int32, ang.shape, 1)
    sign = jnp.where((lane & (HEAD_DIM // 2)) == 0, -1.0, 1.0)
    cos_ref[0] = jnp.cos(ang)
    sin_ref[0] = jnp.sin(ang) * sign


def _rope_tables(positions):
    B, S = positions.shape
    inv_freq = ROPE_THETA ** (-jnp.arange(0, HEAD_DIM, 2, dtype=F32) / HEAD_DIM)
    freq = jnp.tile(inv_freq, LANES // (HEAD_DIM // 2))[None, :]
    out = jax.ShapeDtypeStruct((B, S, LANES), F32)
    return pl.pallas_call(
        _rope_kernel,
        grid=(B, S // ROPE_TILE),
        in_specs=[pl.BlockSpec((1, ROPE_TILE, 1), lambda b, j: (b, j, 0)),
                  pl.BlockSpec((1, LANES), lambda b, j: (0, 0))],
        out_specs=[pl.BlockSpec((1, ROPE_TILE, LANES), lambda b, j: (b, j, 0))] * 2,
        out_shape=[out, out],
        name="rope_tables",
    )(positions.reshape(B, S, 1), freq)


def _mixer_kernel(sinks_ref, x_ref, cos_ref, sin_ref, norm1_ref, w_in_ref, qn_ref, kn_ref,
                  seg_ref, wpool_ref, pscale_ref, wpa_ref, wpb_ref, wpc_ref, vn_ref, ws_ref,
                  bs_ref, wout_ref, o_ref, pool_c, k_c, v_c):
    T = x_ref.shape[1]
    n_blocks = T // ATTN_BLOCK
    j = pl.program_id(1)

    @pl.when(j == 0)
    def _():
        pool_c[...] = jnp.zeros_like(pool_c)
        k_c[...] = jnp.zeros_like(k_c)
        v_c[...] = jnp.zeros_like(v_c)

    x = x_ref[0]
    h = _rms_rows(x, norm1_ref[...]).astype(BF16)
    z = _dot(h, w_in_ref[:, :_C_GATE])
    xp = z[:, _C_POOL:_C_POOL + POOL_WIDTH]
    q = z[:, _C_Q:_C_Q + ATTN_WIDTH]
    k = z[:, _C_K:_C_K + KV_WIDTH]
    v = z[:, _C_V:_C_V + KV_WIDTH]
    su = z[:, _C_SU:_C_SU + SGU_WIDTH]
    sv = z[:, _C_SV:_C_SV + SGU_WIDTH]

    def gate(i):
        lo = _C_GATE + i * D_MODEL
        return jax.nn.sigmoid(_dot(h, w_in_ref[:, lo:lo + D_MODEL]))

    ext = jnp.concatenate([pool_c[...], xp], axis=0)
    pool_c[...] = xp[T - POOL_HALO:, :]
    e1 = ext + pltpu.roll(ext, 1, axis=0)
    e2 = e1 + pltpu.roll(e1, 2, axis=0)
    e3 = e2 + pltpu.roll(e2, 4, axis=0)
    e4 = e3 + pltpu.roll(e3, 8, axis=0)
    lane = lax.broadcasted_iota(jnp.int32, (T, POOL_WIDTH), 1)
    row = lax.broadcasted_iota(jnp.int32, (T, POOL_WIDTH), 0)
    g1, g2, g3 = lane < POOL_GROUP_DIM, lane < 2 * POOL_GROUP_DIM, lane < 3 * POOL_GROUP_DIM
    wsum = jnp.where(g1, e1[POOL_HALO:], jnp.where(g2, e2[POOL_HALO:],
                     jnp.where(g3, e3[POOL_HALO:], e4[POOL_HALO:])))
    win = jnp.where(g1, POOL_WINDOWS[0], jnp.where(g2, POOL_WINDOWS[1],
                    jnp.where(g3, POOL_WINDOWS[2], POOL_WINDOWS[3])))
    count = jnp.minimum(j * T + row + 1, win).astype(F32)
    diff = (wsum / count - xp).astype(BF16)
    mixed = _dot(diff, wpool_ref[...]) * pscale_ref[...]
    merged = gate(0) * _dot(mixed.astype(BF16), wpa_ref[...])

    cos = cos_ref[0]
    sin = sin_ref[0]
    seg = seg_ref[...]

    def norm_rope(t, gain):
        cols = []
        for c in range(t.shape[1] // LANES):
            tc = t[:, c * LANES:(c + 1) * LANES]
            ms = _segsum(tc * tc, seg) * (1.0 / HEAD_DIM)
            tn = tc * lax.rsqrt(ms + EPS) * gain[:, c * LANES:(c + 1) * LANES]
            cols.append(tn * cos + _swap_halves(tn) * sin)
        return cols

    q_cols = norm_rope(q, qn_ref[...] * (HEAD_DIM ** -0.5))
    (k_rot,) = norm_rope(k, kn_ref[...])
    lane1 = lax.broadcasted_iota(jnp.int32, (T, LANES), 1)
    lo1 = lane1 < HEAD_DIM
    q_lo = [jnp.where(lo1, qc, 0.0).astype(BF16) for qc in q_cols]
    q_hi = [jnp.where(lo1, 0.0, qc).astype(BF16) for qc in q_cols]
    k_dup = [t.astype(BF16) for t in _dup_heads(k_rot)]
    v_dup = [t.astype(BF16) for t in _dup_heads(v)]

    qi = lax.broadcasted_iota(jnp.int32, (ATTN_BLOCK, 2 * ATTN_BLOCK), 0)
    kj = lax.broadcasted_iota(jnp.int32, (ATTN_BLOCK, 2 * ATTN_BLOCK), 1)
    band = (kj > qi) & (kj <= qi + ATTN_BLOCK)
    first_band = band & (kj >= jnp.where(j == 0, ATTN_BLOCK, 0))

    attn_blocks = []
    for n in range(n_blocks):
        r0, r1 = n * ATTN_BLOCK, (n + 1) * ATTN_BLOCK
        mask = jnp.concatenate([first_band if n == 0 else band] * Q_PER_KV, axis=0)
        cols = []
        for hd in range(N_KV_HEADS):
            k_prev = k_c[hd] if n == 0 else k_dup[hd][r0 - ATTN_BLOCK:r0]
            v_prev = v_c[hd] if n == 0 else v_dup[hd][r0 - ATTN_BLOCK:r0]
            k_cat = jnp.concatenate([k_prev, k_dup[hd][r0:r1]], axis=0)
            v_cat = jnp.concatenate([v_prev, v_dup[hd][r0:r1]], axis=0)
            c0, c1 = 2 * hd, 2 * hd + 1
            lhs = jnp.concatenate([q_lo[c0][r0:r1], q_hi[c0][r0:r1],
                                   q_lo[c1][r0:r1], q_hi[c1][r0:r1]], axis=0)
            s = lax.dot_general(lhs, k_cat, (((1,), (1,)), ((), ())),
                                preferred_element_type=F32)
            s = jnp.where(mask, s, -jnp.inf)
            sink = jnp.concatenate(
                [jnp.full((ATTN_BLOCK, 1), sinks_ref[Q_PER_KV * hd + g], F32)
                 for g in range(Q_PER_KV)], axis=0)
            m = jnp.maximum(jnp.max(s, axis=1, keepdims=True), sink)
            p = jnp.exp(s - m)
            denom = jnp.sum(p, axis=1, keepdims=True) + jnp.exp(sink - m)
            o = _dot(p.astype(BF16), v_cat) / denom
            lo = lax.broadcasted_iota(jnp.int32, (ATTN_BLOCK, LANES), 1) < HEAD_DIM
            cols.append(jnp.where(lo, o[0:ATTN_BLOCK], o[ATTN_BLOCK:2 * ATTN_BLOCK]))
            cols.append(jnp.where(lo, o[2 * ATTN_BLOCK:3 * ATTN_BLOCK], o[3 * ATTN_BLOCK:]))
        attn_blocks.append(jnp.concatenate(cols, axis=1))
    for hd in range(N_KV_HEADS):
        k_c[hd] = k_dup[hd][T - ATTN_BLOCK:]
        v_c[hd] = v_dup[hd][T - ATTN_BLOCK:]
    attn = jnp.concatenate(attn_blocks, axis=0).astype(BF16)
    merged = merged + gate(1) * _dot(attn, wpb_ref[...])

    u_act = jax.nn.gelu(su)
    v_act = jax.nn.gelu(sv)
    v_cols = []
    for c in range(SGU_WIDTH // LANES):
        vc = v_act[:, c * LANES:(c + 1) * LANES]
        ms = _segsum(vc * vc, seg) * (1.0 / POOL_GROUP_DIM)
        v_cols.append(vc * lax.rsqrt(ms + EPS) * vn_ref[:, c * LANES:(c + 1) * LANES])
    v_lo = [jnp.where(lo1, vc, 0.0).astype(BF16) for vc in v_cols]
    v_hi = [jnp.where(lo1, 0.0, vc).astype(BF16) for vc in v_cols]
    wt = lax.broadcasted_iota(jnp.int32, ws_ref.shape, 0)
    wsrc = lax.broadcasted_iota(jnp.int32, ws_ref.shape, 1) & (CHUNK - 1)
    w_tril = jnp.where(wsrc <= wt, ws_ref[...], jnp.zeros_like(ws_ref))
    gated = []
    for n in range(T // CHUNK):
        r0, r1 = n * CHUNK, (n + 1) * CHUNK
        s_cols = []
        for c in range(SGU_WIDTH // LANES):
            rhs = jnp.concatenate([v_lo[c][r0:r1], v_hi[c][r0:r1]], axis=0)
            s_cols.append(_dot(w_tril[:, 2 * c * CHUNK:2 * (c + 1) * CHUNK], rhs))
        s = jnp.concatenate(s_cols, axis=1) + bs_ref[...]
        gated.append(u_act[r0:r1] * s)
    sgu = jnp.concatenate(gated, axis=0).astype(BF16)
    merged = merged + gate(2) * _dot(sgu, wpc_ref[...])

    o_ref[0] = x + _dot(merged.astype(BF16), wout_ref[...])


def _mixer_layer(x, cos, sin, sinks, norm1, w_in, qn, kn, seg, wpool, pscale, wpa, wpb, wpc, vn,
                 ws, bs, wout):
    B, S, D = x.shape
    T = MIXER_TILE
    row_spec = lambda w: pl.BlockSpec((1, T, w), lambda b, j: (b, j, 0))
    consts = (norm1, w_in, qn, kn, seg, wpool, pscale, wpa, wpb, wpc, vn, ws, bs, wout)
    return pl.pallas_call(
        _mixer_kernel,
        grid=(B, S // T),
        in_specs=[pl.BlockSpec(memory_space=pltpu.SMEM), row_spec(D), row_spec(LANES),
                  row_spec(LANES)] + [_const_spec(c.shape) for c in consts],
        out_specs=row_spec(D),
        out_shape=jax.ShapeDtypeStruct(x.shape, x.dtype),
        scratch_shapes=[pltpu.VMEM((POOL_HALO, POOL_WIDTH), F32),
                        pltpu.VMEM((N_KV_HEADS, ATTN_BLOCK, LANES), BF16),
                        pltpu.VMEM((N_KV_HEADS, ATTN_BLOCK, LANES), BF16)],
        compiler_params=pltpu.CompilerParams(
            dimension_semantics=("arbitrary", "arbitrary"), vmem_limit_bytes=VMEM_LIMIT),
        name="mixer",
    )(sinks, x, cos, sin, *consts)


def _ffn_kernel(x_ref, norm2_ref, wup_ref, cw_ref, cb_ref, wdown_ref, o_ref, tail):
    T = x_ref.shape[1]
    j = pl.program_id(1)

    @pl.when(j == 0)
    def _():
        tail[...] = jnp.zeros_like(tail)

    x = x_ref[0]
    h = _rms_rows(x, norm2_ref[...]).astype(BF16)

    def conv(lo):
        up = _dot(h, wup_ref[:, lo:lo + FFN_CHUNK])
        ext = jnp.concatenate([tail[:, lo:lo + FFN_CHUNK], up], axis=0)
        tail[:, lo:lo + FFN_CHUNK] = up[T - CONV_HALO:, :]
        w = cw_ref[:, lo:lo + FFN_CHUNK]
        y = (ext * w[2:3] + pltpu.roll(ext, 1, axis=0) * w[1:2]
             + pltpu.roll(ext, 2, axis=0) * w[0:1])
        return y[CONV_HALO:] + cb_ref[:, lo:lo + FFN_CHUNK]

    acc = x
    for c in range(D_FF // FFN_CHUNK):
        lo = c * FFN_CHUNK
        act = (jax.nn.silu(conv(lo)) * conv(D_FF + lo)).astype(BF16)
        acc = acc + _dot(act, wdown_ref[lo:lo + FFN_CHUNK, :])
    o_ref[0] = acc


def _ffn_layer(x, norm2, wup, cw, cb, wdown):
    B, S, D = x.shape
    T = FFN_TILE
    row_spec = pl.BlockSpec((1, T, D), lambda b, j: (b, j, 0))
    consts = (norm2, wup, cw, cb, wdown)
    return pl.pallas_call(
        _ffn_kernel,
        grid=(B, S // T),
        in_specs=[row_spec] + [_const_spec(c.shape) for c in consts],
        out_specs=row_spec,
        out_shape=jax.ShapeDtypeStruct(x.shape, x.dtype),
        scratch_shapes=[pltpu.VMEM((CONV_HALO, 2 * D_FF), F32)],
        compiler_params=pltpu.CompilerParams(
            dimension_semantics=("arbitrary", "arbitrary"), vmem_limit_bytes=VMEM_LIMIT),
        name="ffn",
    )(x, *consts)


def _block_diag(blocks):
    G, a, b = blocks.shape
    eye = jnp.eye(G, dtype=blocks.dtype)
    return (eye[:, None, :, None] * blocks[:, :, None, :]).reshape(G * a, G * b)


def kernel(x, positions, norm1, w_in, q_norm, k_norm, sinks, w_pool, pool_scale, sgu_v_norm, w_s,
           b_s, w_proj_a, w_proj_b, w_proj_c, w_out, norm2, w_up, conv_w, conv_b, w_down):
    cos, sin = _rope_tables(positions)
    seg = _block_diag(jnp.ones((LANES // HEAD_DIM, HEAD_DIM, HEAD_DIM), BF16))
    row = lambda a: a[None, :].astype(F32)
    for l in range(DEPTH):
        x = _mixer_layer(
            x, cos, sin, sinks[l], row(norm1[l]), w_in[l].astype(BF16),
            row(jnp.tile(q_norm[l], N_Q_HEADS)), row(jnp.tile(k_norm[l], N_KV_HEADS)), seg,
            _block_diag(w_pool[l]).astype(BF16), row(pool_scale[l]),
            w_proj_a[l].astype(BF16), w_proj_b[l].astype(BF16), w_proj_c[l].astype(BF16),
            row(jnp.tile(sgu_v_norm[l], SGU_WIDTH // POOL_GROUP_DIM)),
            jnp.transpose(w_s[l], (1, 0, 2)).reshape(CHUNK, -1).astype(BF16),
            jnp.repeat(b_s[l].T, POOL_GROUP_DIM, axis=1),
            w_out[l].astype(BF16))
        x = _ffn_layer(x, row(norm2[l]), w_up[l].astype(BF16), conv_w[l], row(conv_b[l]),
                       w_down[l].astype(BF16))
    return x
```

```python
import functools

import jax
import jax.numpy as jnp
from jax import lax
from jax.experimental import pallas as pl
from jax.experimental.pallas import tpu as pltpu

D_MODEL = 1024
DEPTH = 2
HEAD_DIM = 64
POOL_WINDOWS = (2, 4, 8, 16)
POOL_GROUP_DIM = 64
POOL_WIDTH = 256
N_Q_HEADS = 8
N_KV_HEADS = 2
Q_PER_KV = 4
ATTN_BLOCK = 128
ATTN_WIDTH = 512
KV_WIDTH = 128
CHUNK = 128
SGU_WIDTH = 256
N_BRANCHES = 3
D_FF = 2816
CONV_WIDTH = 3
ROPE_THETA = 10000.0
EPS = 1e-6

_C_POOL = 0
_C_Q = _C_POOL + POOL_WIDTH
_C_K = _C_Q + ATTN_WIDTH
_C_V = _C_K + KV_WIDTH
_C_SU = _C_V + KV_WIDTH
_C_SV = _C_SU + SGU_WIDTH
_C_GATE = _C_SV + SGU_WIDTH
IN_COLS = _C_GATE + N_BRANCHES * D_MODEL

LANES = 128
SUBLANES = 8
POOL_HALO = 16
CONV_HALO = SUBLANES
MIXER_TILE = 256
FFN_TILE = 256
FFN_CHUNK = 256
ROPE_TILE = 2048
VMEM_LIMIT = 56 * 1024 * 1024

F32 = jnp.float32
BF16 = jnp.bfloat16


def _dot(a, b):
    return jnp.dot(a, b, preferred_element_type=F32)


def _const_spec(shape):
    zeros = (0,) * len(shape)
    return pl.BlockSpec(shape, lambda *_: zeros, pipeline_mode=pl.Buffered(1))


def _rms_rows(x, gain):
    ms = jnp.mean(x * x, axis=-1, keepdims=True)
    return x * lax.rsqrt(ms + EPS) * gain


def _segsum(v, seg):
    hi = v.astype(BF16)
    lo = (v - hi.astype(F32)).astype(BF16)
    return _dot(hi, seg) + _dot(lo, seg)


def _swap_halves(x):
    lane = lax.broadcasted_iota(jnp.int32, x.shape, 1)
    fwd = pltpu.roll(x, LANES - HEAD_DIM // 2, axis=1)
    bwd = pltpu.roll(x, HEAD_DIM // 2, axis=1)
    return jnp.where((lane & (HEAD_DIM // 2)) == 0, fwd, bwd)


def _dup_heads(x):
    lane = lax.broadcasted_iota(jnp.int32, x.shape, 1)
    rolled = pltpu.roll(x, HEAD_DIM, axis=1)
    lo = lane < HEAD_DIM
    return jnp.where(lo, x, rolled), jnp.where(lo, rolled, x)


def _rope_kernel(pos_ref, freq_ref, cos_ref, sin_ref):
    ang = pos_ref[0].astype(F32) * freq_ref[...]
    lane = lax.broadcasted_iota(jnp.int32, ang.shape, 1)
    sign = jnp.where((lane & (HEAD_DIM // 2)) == 0, -1.0, 1.0)
    cos_ref[0] = jnp.cos(ang)
    sin_ref[0] = jnp.sin(ang) * sign


def _rope_tables(positions):
    B, S = positions.shape
    inv_freq = ROPE_THETA ** (-jnp.arange(0, HEAD_DIM, 2, dtype=F32) / HEAD_DIM)
    freq = jnp.tile(inv_freq, LANES // (HEAD_DIM // 2))[None, :]
    out = jax.ShapeDtypeStruct((B, S, LANES), F32)
    return pl.pallas_call(
        _rope_kernel,
        grid=(B, S // ROPE_TILE),
        in_specs=[pl.BlockSpec((1, ROPE_TILE, 1), lambda b, j: (b, j, 0)),
                  pl.BlockSpec((1, LANES), lambda b, j: (0, 0))],
        out_specs=[pl.BlockSpec((1, ROPE_TILE, LANES), lambda b, j: (b, j, 0))] * 2,
        out_shape=[out, out],
        name="rope_tables",
    )(positions.reshape(B, S, 1), freq)


def _mixer_kernel(sinks_ref, x_ref, cos_ref, sin_ref, norm1_ref, w_in_ref, qn_ref, kn_ref,
                  seg_ref, wpool_ref, pscale_ref, wpa_ref, wpb_ref, wpc_ref, vn_ref, ws_ref,
                  bs_ref, wout_ref, o_ref, pool_c, k_c, v_c):
    T = x_ref.shape[1]
    n_blocks = T // ATTN_BLOCK
    j = pl.program_id(1)

    @pl.when(j == 0)
    def _():
        pool_c[...] = jnp.zeros_like(pool_c)
        k_c[...] = jnp.zeros_like(k_c)
        v_c[...] = jnp.zeros_like(v_c)

    x = x_ref[0]
    h = _rms_rows(x, norm1_ref[...]).astype(BF16)
    z = _dot(h, w_in_ref[:, :_C_GATE])
    xp = z[:, _C_POOL:_C_POOL + POOL_WIDTH]
    q = z[:, _C_Q:_C_Q + ATTN_WIDTH]
    k = z[:, _C_K:_C_K + KV_WIDTH]
    v = z[:, _C_V:_C_V + KV_WIDTH]
    su = z[:, _C_SU:_C_SU + SGU_WIDTH]
    sv = z[:, _C_SV:_C_SV + SGU_WIDTH]

    def gate(i):
        lo = _C_GATE + i * D_MODEL
        return jax.nn.sigmoid(_dot(h, w_in_ref[:, lo:lo + D_MODEL]))

    ext = jnp.concatenate([pool_c[...], xp], axis=0)
    pool_c[...] = xp[T - POOL_HALO:, :]
    e1 = ext + pltpu.roll(ext, 1, axis=0)
    e2 = e1 + pltpu.roll(e1, 2, axis=0)
    e3 = e2 + pltpu.roll(e2, 4, axis=0)
    e4 = e3 + pltpu.roll(e3, 8, axis=0)
    lane = lax.broadcasted_iota(jnp.int32, (T, POOL_WIDTH), 1)
    row = lax.broadcasted_iota(jnp.int32, (T, POOL_WIDTH), 0)
    g1, g2, g3 = lane < POOL_GROUP_DIM, lane < 2 * POOL_GROUP_DIM, lane < 3 * POOL_GROUP_DIM
    wsum = jnp.where(g1, e1[POOL_HALO:], jnp.where(g2, e2[POOL_HALO:],
                     jnp.where(g3, e3[POOL_HALO:], e4[POOL_HALO:])))
    win = jnp.where(g1, POOL_WINDOWS[0], jnp.where(g2, POOL_WINDOWS[1],
                    jnp.where(g3, POOL_WINDOWS[2], POOL_WINDOWS[3])))
    count = jnp.minimum(j * T + row + 1, win).astype(F32)
    diff = (wsum / count - xp).astype(BF16)
    mixed = _dot(diff, wpool_ref[...]) * pscale_ref[...]
    merged = gate(0) * _dot(mixed.astype(BF16), wpa_ref[...])

    cos = cos_ref[0]
    sin = sin_ref[0]
    seg = seg_ref[...]

    def norm_rope(t, gain):
        cols = []
        for c in range(t.shape[1] // LANES):
            tc = t[:, c * LANES:(c + 1) * LANES]
            ms = _segsum(tc * tc, seg) * (1.0 / HEAD_DIM)
            tn = tc * lax.rsqrt(ms + EPS) * gain[:, c * LANES:(c + 1) * LANES]
            cols.append(tn * cos + _swap_halves(tn) * sin)
        return cols

    q_cols = norm_rope(q, qn_ref[...] * (HEAD_DIM ** -0.5))
    (k_rot,) = norm_rope(k, kn_ref[...])
    lane1 = lax.broadcasted_iota(jnp.int32, (T, LANES), 1)
    lo1 = lane1 < HEAD_DIM
    q_lo = [jnp.where(lo1, qc, 0.0).astype(BF16) for qc in q_cols]
    q_hi = [jnp.where(lo1, 0.0, qc).astype(BF16) for qc in q_cols]
    k_dup = [t.astype(BF16) for t in _dup_heads(k_rot)]
    v_dup = [t.astype(BF16) for t in _dup_heads(v)]

    qi = lax.broadcasted_iota(jnp.int32, (ATTN_BLOCK, 2 * ATTN_BLOCK), 0)
    kj = lax.broadcasted_iota(jnp.int32, (ATTN_BLOCK, 2 * ATTN_BLOCK), 1)
    band = (kj > qi) & (kj <= qi + ATTN_BLOCK)
    first_band = band & (kj >= jnp.where(j == 0, ATTN_BLOCK, 0))

    attn_blocks = []
    for n in range(n_blocks):
        r0, r1 = n * ATTN_BLOCK, (n + 1) * ATTN_BLOCK
        mask = jnp.concatenate([first_band if n == 0 else band] * Q_PER_KV, axis=0)
        cols = []
        for hd in range(N_KV_HEADS):
            k_prev = k_c[hd] if n == 0 else k_dup[hd][r0 - ATTN_BLOCK:r0]
            v_prev = v_c[hd] if n == 0 else v_dup[hd][r0 - ATTN_BLOCK:r0]
            k_cat = jnp.concatenate([k_prev, k_dup[hd][r0:r1]], axis=0)
            v_cat = jnp.concatenate([v_prev, v_dup[hd][r0:r1]], axis=0)
            c0, c1 = 2 * hd, 2 * hd + 1
            lhs = jnp.concatenate([q_lo[c0][r0:r1], q_hi[c0][r0:r1],
                                   q_lo[c1][r0:r1], q_hi[c1][r0:r1]], axis=0)
            s = lax.dot_general(lhs, k_cat, (((1,), (1,)), ((), ())),
                                preferred_element_type=F32)
            s = jnp.where(mask, s, -jnp.inf)
            sink = jnp.concatenate(
                [jnp.full((ATTN_BLOCK, 1), sinks_ref[Q_PER_KV * hd + g], F32)
                 for g in range(Q_PER_KV)], axis=0)
            m = jnp.maximum(jnp.max(s, axis=1, keepdims=True), sink)
            p = jnp.exp(s - m)
            denom = jnp.sum(p, axis=1, keepdims=True) + jnp.exp(sink - m)
            o = _dot(p.astype(BF16), v_cat) / denom
            lo = lax.broadcasted_iota(jnp.int32, (ATTN_BLOCK, LANES), 1) < HEAD_DIM
            cols.append(jnp.where(lo, o[0:ATTN_BLOCK], o[ATTN_BLOCK:2 * ATTN_BLOCK]))
            cols.append(jnp.where(lo, o[2 * ATTN_BLOCK:3 * ATTN_BLOCK], o[3 * ATTN_BLOCK:]))
        attn_blocks.append(jnp.concatenate(cols, axis=1))
    for hd in range(N_KV_HEADS):
        k_c[hd] = k_dup[hd][T - ATTN_BLOCK:]
        v_c[hd] = v_dup[hd][T - ATTN_BLOCK:]
    attn = jnp.concatenate(attn_blocks, axis=0).astype(BF16)
    merged = merged + gate(1) * _dot(attn, wpb_ref[...])

    u_act = jax.nn.gelu(su)
    v_act = jax.nn.gelu(sv)
    v_cols = []
    for c in range(SGU_WIDTH // LANES):
        vc = v_act[:, c * LANES:(c + 1) * LANES]
        ms = _segsum(vc * vc, seg) * (1.0 / POOL_GROUP_DIM)
        v_cols.append(vc * lax.rsqrt(ms + EPS) * vn_ref[:, c * LANES:(c + 1) * LANES])
    v_lo = [jnp.where(lo1, vc, 0.0).astype(BF16) for vc in v_cols]
    v_hi = [jnp.where(lo1, 0.0, vc).astype(BF16) for vc in v_cols]
    wt = lax.broadcasted_iota(jnp.int32, ws_ref.shape, 0)
    wsrc = lax.broadcasted_iota(jnp.int32, ws_ref.shape, 1) & (CHUNK - 1)
    w_tril = jnp.where(wsrc <= wt, ws_ref[...], jnp.zeros_like(ws_ref))
    gated = []
    for n in range(T // CHUNK):
        r0, r1 = n * CHUNK, (n + 1) * CHUNK
        s_cols = []
        for c in range(SGU_WIDTH // LANES):
            rhs = jnp.concatenate([v_lo[c][r0:r1], v_hi[c][r0:r1]], axis=0)
            s_cols.append(_dot(w_tril[:, 2 * c * CHUNK:2 * (c + 1) * CHUNK], rhs))
        s = jnp.concatenate(s_cols, axis=1) + bs_ref[...]
        gated.append(u_act[r0:r1] * s)
    sgu = jnp.concatenate(gated, axis=0).astype(BF16)
    merged = merged + gate(2) * _dot(sgu, wpc_ref[...])

    o_ref[0] = x + _dot(merged.astype(BF16), wout_ref[...])


def _mixer_layer(x, cos, sin, sinks, norm1, w_in, qn, kn, seg, wpool, pscale, wpa, wpb, wpc, vn,
                 ws, bs, wout):
    B, S, D = x.shape
    T = MIXER_TILE
    row_spec = lambda w: pl.BlockSpec((1, T, w), lambda b, j: (b, j, 0))
    consts = (norm1, w_in, qn, kn, seg, wpool, pscale, wpa, wpb, wpc, vn, ws, bs, wout)
    return pl.pallas_call(
        _mixer_kernel,
        grid=(B, S // T),
        in_specs=[pl.BlockSpec(memory_space=pltpu.SMEM), row_spec(D), row_spec(LANES),
                  row_spec(LANES)] + [_const_spec(c.shape) for c in consts],
        out_specs=row_spec(D),
        out_shape=jax.ShapeDtypeStruct(x.shape, x.dtype),
        scratch_shapes=[pltpu.VMEM((POOL_HALO, POOL_WIDTH), F32),
                        pltpu.VMEM((N_KV_HEADS, ATTN_BLOCK, LANES), BF16),
                        pltpu.VMEM((N_KV_HEADS, ATTN_BLOCK, LANES), BF16)],
        compiler_params=pltpu.CompilerParams(
            dimension_semantics=("arbitrary", "arbitrary"), vmem_limit_bytes=VMEM_LIMIT),
        name="mixer",
    )(sinks, x, cos, sin, *consts)


FFN_CHUNKS = D_FF // FFN_CHUNK


def _ffn_kernel(x_ref, norm2_ref, wup_ref, cw_ref, cb_ref, wdown_ref, o_ref, tail, act_buf, stage):
    T = x_ref.shape[1]
    R = T // SUBLANES
    W = 2 * FFN_CHUNK
    j = pl.program_id(1)

    @pl.when(j == 0)
    def _():
        tail[...] = jnp.zeros_like(tail)

    n_col = x_ref.shape[2] // LANES
    for cb in range(n_col):
        stage[cb] = x_ref[0, :, cb * LANES:(cb + 1) * LANES]
    x = jnp.concatenate(
        [jnp.concatenate([stage[cb, pl.ds(p, R, stride=SUBLANES), :] for cb in range(n_col)], axis=1)
         for p in range(SUBLANES)], axis=0)
    h = _rms_rows(x, norm2_ref[...]).astype(BF16)
    first_row = lax.broadcasted_iota(jnp.int32, (R, W), 0) == 0

    def up(c):
        return _dot(h, wup_ref[:, c * W:(c + 1) * W])

    def act(c, u):
        w = cw_ref[:, c * W:(c + 1) * W]
        u6, u7 = u[6 * R:7 * R], u[7 * R:]
        s6 = jnp.where(first_row, tail[c, 0, SUBLANES - 1:, :], pltpu.roll(u6, 1, axis=0))
        s7 = jnp.where(first_row, tail[c, 1, SUBLANES - 1:, :], pltpu.roll(u7, 1, axis=0))
        tail[c, 0] = u6[R - SUBLANES:]
        tail[c, 1] = u7[R - SUBLANES:]
        prev1 = jnp.concatenate([s7, u[:7 * R]], axis=0)
        prev2 = jnp.concatenate([s6, s7, u[:6 * R]], axis=0)
        y = u * w[2:3] + prev1 * w[1:2] + prev2 * w[0:1] + cb_ref[:, c * W:(c + 1) * W]
        a = jax.nn.silu(y[:, :FFN_CHUNK]) * y[:, FFN_CHUNK:]
        act_buf[:, c * FFN_CHUNK:(c + 1) * FFN_CHUNK] = a.astype(BF16)

    u_next = up(0)
    for c in range(FFN_CHUNKS):
        u_cur = u_next
        if c + 1 < FFN_CHUNKS:
            u_next = up(c + 1)
        act(c, u_cur)
    out = x + _dot(act_buf[...], wdown_ref[...])
    for cb in range(n_col):
        for p in range(SUBLANES):
            stage[cb, pl.ds(p, R, stride=SUBLANES), :] = out[p * R:(p + 1) * R,
                                                             cb * LANES:(cb + 1) * LANES]
        o_ref[0, :, cb * LANES:(cb + 1) * LANES] = stage[cb]


def _ffn_layer(x, norm2, wup, cw, cb, wdown):
    B, S, D = x.shape
    T = FFN_TILE
    row_spec = pl.BlockSpec((1, T, D), lambda b, j: (b, j, 0))
    consts = (norm2, wup, cw, cb, wdown)
    return pl.pallas_call(
        _ffn_kernel,
        grid=(B, S // T),
        in_specs=[row_spec] + [_const_spec(c.shape) for c in consts],
        out_specs=row_spec,
        out_shape=jax.ShapeDtypeStruct(x.shape, x.dtype),
        scratch_shapes=[pltpu.VMEM((FFN_CHUNKS, CONV_WIDTH - 1, SUBLANES, 2 * FFN_CHUNK), F32),
                        pltpu.VMEM((T, D_FF), BF16),
                        pltpu.VMEM((D // LANES, T, LANES), F32)],
        compiler_params=pltpu.CompilerParams(
            dimension_semantics=("arbitrary", "arbitrary"), vmem_limit_bytes=VMEM_LIMIT),
        name="ffn",
    )(x, *consts)


def _pair_chunks(a):
    lead = a.shape[:-1]
    a = a.reshape(*lead, 2, FFN_CHUNKS, FFN_CHUNK)
    return jnp.swapaxes(a, -3, -2).reshape(*lead, 2 * D_FF)


def _block_diag(blocks):
    G, a, b = blocks.shape
    eye = jnp.eye(G, dtype=blocks.dtype)
    return (eye[:, None, :, None] * blocks[:, :, None, :]).reshape(G * a, G * b)


def kernel(x, positions, norm1, w_in, q_norm, k_norm, sinks, w_pool, pool_scale, sgu_v_norm, w_s,
           b_s, w_proj_a, w_proj_b, w_proj_c, w_out, norm2, w_up, conv_w, conv_b, w_down):
    cos, sin = _rope_tables(positions)
    seg = _block_diag(jnp.ones((LANES // HEAD_DIM, HEAD_DIM, HEAD_DIM), BF16))
    row = lambda a: a[None, :].astype(F32)
    for l in range(DEPTH):
        x = _mixer_layer(
            x, cos, sin, sinks[l], row(norm1[l]), w_in[l].astype(BF16),
            row(jnp.tile(q_norm[l], N_Q_HEADS)), row(jnp.tile(k_norm[l], N_KV_HEADS)), seg,
            _block_diag(w_pool[l]).astype(BF16), row(pool_scale[l]),
            w_proj_a[l].astype(BF16), w_proj_b[l].astype(BF16), w_proj_c[l].astype(BF16),
            row(jnp.tile(sgu_v_norm[l], SGU_WIDTH // POOL_GROUP_DIM)),
            jnp.transpose(w_s[l], (1, 0, 2)).reshape(CHUNK, -1).astype(BF16),
            jnp.repeat(b_s[l].T, POOL_GROUP_DIM, axis=1),
            w_out[l].astype(BF16))
        x = _ffn_layer(x, row(norm2[l]), _pair_chunks(w_up[l]).astype(BF16),
                       _pair_chunks(conv_w[l]), row(_pair_chunks(conv_b[l])),
                       w_down[l].astype(BF16))
    return x
```

```python
import jax
import jax.numpy as jnp
from jax import lax
from jax.experimental import pallas as pl
from jax.experimental.pallas import tpu as pltpu

D_MODEL = 1024
DEPTH = 2
HEAD_DIM = 64
POOL_WINDOWS = (2, 4, 8, 16)
POOL_GROUP_DIM = 64
POOL_WIDTH = 256
N_Q_HEADS = 8
N_KV_HEADS = 2
Q_PER_KV = 4
ATTN_BLOCK = 128
ATTN_WIDTH = 512
KV_WIDTH = 128
CHUNK = 128
SGU_WIDTH = 256
N_BRANCHES = 3
D_FF = 2816
CONV_WIDTH = 3
ROPE_THETA = 10000.0
EPS = 1e-6

_C_POOL = 0
_C_Q = _C_POOL + POOL_WIDTH
_C_K = _C_Q + ATTN_WIDTH
_C_V = _C_K + KV_WIDTH
_C_SU = _C_V + KV_WIDTH
_C_SV = _C_SU + SGU_WIDTH
_C_GATE = _C_SV + SGU_WIDTH
IN_COLS = _C_GATE + N_BRANCHES * D_MODEL

LANES = 128
SUBLANES = 8
POOL_HALO = 16
MIXER_TILE = 512
FFN_TILE = 512
FFN_CHUNK = 256
FFN_CHUNKS = D_FF // FFN_CHUNK
ROPE_TILE = 2048
VMEM_LIMIT = 56 * 1024 * 1024

F32 = jnp.float32
BF16 = jnp.bfloat16


def _dot(a, b):
    return jnp.dot(a, b, preferred_element_type=F32)


def _const_spec(shape):
    zeros = (0,) * len(shape)
    return pl.BlockSpec(shape, lambda *_: zeros, pipeline_mode=pl.Buffered(1))


def _rms_rows(x, gain):
    ms = jnp.mean(x * x, axis=-1, keepdims=True)
    return x * lax.rsqrt(ms + EPS) * gain


def _swap_halves(x):
    lane = lax.broadcasted_iota(jnp.int32, x.shape, 1)
    fwd = pltpu.roll(x, LANES - HEAD_DIM // 2, axis=1)
    bwd = pltpu.roll(x, HEAD_DIM // 2, axis=1)
    return jnp.where((lane & (HEAD_DIM // 2)) == 0, fwd, bwd)


def _dup_heads(x):
    lane = lax.broadcasted_iota(jnp.int32, x.shape, 1)
    rolled = pltpu.roll(x, HEAD_DIM, axis=1)
    lo = lane < HEAD_DIM
    return jnp.where(lo, x, rolled), jnp.where(lo, rolled, x)


def _rope_kernel(pos_ref, freq_ref, cos_ref, sin_ref):
    ang = pos_ref[0].astype(F32) * freq_ref[...]
    lane = lax.broadcasted_iota(jnp.int32, ang.shape, 1)
    sign = jnp.where((lane & (HEAD_DIM // 2)) == 0, -1.0, 1.0)
    cos_ref[0] = jnp.cos(ang)
    sin_ref[0] = jnp.sin(ang) * sign


def _rope_tables(positions):
    B, S = positions.shape
    inv_freq = ROPE_THETA ** (-jnp.arange(0, HEAD_DIM, 2, dtype=F32) / HEAD_DIM)
    freq = jnp.tile(inv_freq, LANES // (HEAD_DIM // 2))[None, :]
    out = jax.ShapeDtypeStruct((B, S, LANES), F32)
    return pl.pallas_call(
        _rope_kernel,
        grid=(B, S // ROPE_TILE),
        in_specs=[pl.BlockSpec((1, ROPE_TILE, 1), lambda b, j: (b, j, 0)),
                  pl.BlockSpec((1, LANES), lambda b, j: (0, 0))],
        out_specs=[pl.BlockSpec((1, ROPE_TILE, LANES), lambda b, j: (b, j, 0))] * 2,
        out_shape=[out, out],
        name="rope_tables",
    )(positions.reshape(B, S, 1), freq)


def _mixer_kernel(sinks_ref, x_ref, cos_ref, sin_ref, norm1_ref, w_in_ref, qn_ref, kn_ref,
                  seg_ref, wpool_ref, pscale_ref, wpa_ref, wpb_ref, wpc_ref, vn_ref, ws_ref,
                  bs_ref, wout_ref, o_ref, pool_c, k_c, v_c):
    T = x_ref.shape[1]
    n_blocks = T // ATTN_BLOCK
    j = pl.program_id(1)

    @pl.when(j == 0)
    def _():
        pool_c[...] = jnp.zeros_like(pool_c)
        k_c[...] = jnp.zeros_like(k_c)
        v_c[...] = jnp.zeros_like(v_c)

    def gate(i):
        lo = _C_GATE + i * D_MODEL
        return jax.nn.sigmoid(_dot(h, w_in_ref[:, lo:lo + D_MODEL]))

    def cols_of(t):
        return [t[:, c * LANES:(c + 1) * LANES] for c in range(t.shape[1] // LANES)]

    def split_bf16(t):
        hi = t.astype(BF16)
        return hi, (t - hi.astype(F32)).astype(BF16)

    x = x_ref[0]
    h = _rms_rows(x, norm1_ref[...]).astype(BF16)
    z = _dot(h, w_in_ref[:, :_C_GATE])
    xp = z[:, _C_POOL:_C_POOL + POOL_WIDTH]
    q = z[:, _C_Q:_C_Q + ATTN_WIDTH]
    k = z[:, _C_K:_C_K + KV_WIDTH]
    v = z[:, _C_V:_C_V + KV_WIDTH]
    su = z[:, _C_SU:_C_SU + SGU_WIDTH]
    sv = z[:, _C_SV:_C_SV + SGU_WIDTH]
    g_a = gate(0)

    ext = jnp.concatenate([pool_c[...], xp], axis=0)
    pool_c[...] = xp[T - POOL_HALO:, :]
    e1 = ext + pltpu.roll(ext, 1, axis=0)
    e2 = e1 + pltpu.roll(e1, 2, axis=0)
    e3 = e2 + pltpu.roll(e2, 4, axis=0)
    e4 = e3 + pltpu.roll(e3, 8, axis=0)
    lane = lax.broadcasted_iota(jnp.int32, (T, POOL_WIDTH), 1)
    row = lax.broadcasted_iota(jnp.int32, (T, POOL_WIDTH), 0)
    g1, g2, g3 = lane < POOL_GROUP_DIM, lane < 2 * POOL_GROUP_DIM, lane < 3 * POOL_GROUP_DIM
    wsum = jnp.where(g1, e1[POOL_HALO:], jnp.where(g2, e2[POOL_HALO:],
                     jnp.where(g3, e3[POOL_HALO:], e4[POOL_HALO:])))
    win = jnp.where(g1, POOL_WINDOWS[0], jnp.where(g2, POOL_WINDOWS[1],
                    jnp.where(g3, POOL_WINDOWS[2], POOL_WINDOWS[3])))
    count = jnp.minimum(j * T + row + 1, win).astype(F32)
    diff = (wsum / count - xp).astype(BF16)
    u_act = jax.nn.gelu(su)
    v_act = jax.nn.gelu(sv)
    sq_parts = [split_bf16(t * t) for t in (q[:, :2 * LANES], q[:, 2 * LANES:], v_act)]
    kk_parts = split_bf16(k * k)

    seg = seg_ref[...]
    ms_q0, ms_q1, ms_v = [(_dot(hi, seg) + _dot(lo, seg)) * (1.0 / HEAD_DIM) for hi, lo in sq_parts]
    ms_k = (_dot(kk_parts[0], seg_ref[:LANES, :LANES])
            + _dot(kk_parts[1], seg_ref[:LANES, :LANES])) * (1.0 / HEAD_DIM)
    mixed = (_dot(diff, wpool_ref[...]) * pscale_ref[...]).astype(BF16)
    g_b = gate(1)

    cos = cos_ref[0]
    sin = sin_ref[0]
    ms_q = jnp.concatenate([ms_q0, ms_q1], axis=1)
    q_n = q * lax.rsqrt(ms_q + EPS) * (qn_ref[...] * (HEAD_DIM ** -0.5))
    k_n = k * lax.rsqrt(ms_k + EPS) * kn_ref[...]
    q_cols = [tc * cos + _swap_halves(tc) * sin for tc in cols_of(q_n)]
    k_rot = k_n * cos + _swap_halves(k_n) * sin
    lo1 = lax.broadcasted_iota(jnp.int32, (T, LANES), 1) < HEAD_DIM
    q_lo = [jnp.where(lo1, qc, 0.0).astype(BF16) for qc in q_cols]
    q_hi = [jnp.where(lo1, 0.0, qc).astype(BF16) for qc in q_cols]
    k_dup = [t.astype(BF16) for t in _dup_heads(k_rot)]
    v_n = v_act * lax.rsqrt(ms_v + EPS) * vn_ref[...]
    v_lo = [jnp.where(lo1, vc, 0.0).astype(BF16) for vc in cols_of(v_n)]
    v_hi = [jnp.where(lo1, 0.0, vc).astype(BF16) for vc in cols_of(v_n)]
    wt = lax.broadcasted_iota(jnp.int32, ws_ref.shape, 0)
    wsrc = lax.broadcasted_iota(jnp.int32, ws_ref.shape, 1) & (CHUNK - 1)
    w_tril = jnp.where(wsrc <= wt, ws_ref[...], jnp.zeros_like(ws_ref))

    y_a = _dot(mixed, wpa_ref[...])
    scores = {}
    for n in range(n_blocks):
        r0, r1 = n * ATTN_BLOCK, (n + 1) * ATTN_BLOCK
        for hd in range(N_KV_HEADS):
            k_prev = k_c[hd] if n == 0 else k_dup[hd][r0 - ATTN_BLOCK:r0]
            k_cat = jnp.concatenate([k_prev, k_dup[hd][r0:r1]], axis=0)
            c0, c1 = 2 * hd, 2 * hd + 1
            q_all = jnp.concatenate([q_lo[c0][r0:r1], q_hi[c0][r0:r1],
                                     q_lo[c1][r0:r1], q_hi[c1][r0:r1]], axis=0)
            scores[n, hd] = lax.dot_general(k_cat, q_all, (((1,), (1,)), ((), ())),
                                            preferred_element_type=F32)
    g_c = gate(2)

    kj = lax.broadcasted_iota(jnp.int32, (2 * ATTN_BLOCK, ATTN_BLOCK), 0)
    qi = lax.broadcasted_iota(jnp.int32, (2 * ATTN_BLOCK, ATTN_BLOCK), 1)
    band = (kj > qi) & (kj <= qi + ATTN_BLOCK)
    first_band = band & (kj >= jnp.where(j == 0, ATTN_BLOCK, 0))
    probs, denoms = {}, {}
    for (n, hd), s in scores.items():
        mask = jnp.concatenate([first_band if n == 0 else band] * Q_PER_KV, axis=1)
        s = jnp.where(mask, s, -jnp.inf)
        sink = jnp.concatenate(
            [jnp.full((1, ATTN_BLOCK), sinks_ref[Q_PER_KV * hd + g], F32)
             for g in range(Q_PER_KV)], axis=1)
        m = jnp.maximum(jnp.max(s, axis=0, keepdims=True), sink)
        p = jnp.exp(s - m)
        denoms[n, hd] = jnp.sum(p, axis=0, keepdims=True) + jnp.exp(sink - m)
        probs[n, hd] = p.astype(BF16)

    v_t = v.T.astype(BF16)
    head_rows = [[None] * n_blocks for _ in range(N_Q_HEADS)]
    for n in range(n_blocks):
        r0, r1 = n * ATTN_BLOCK, (n + 1) * ATTN_BLOCK
        for hd in range(N_KV_HEADS):
            v_prev = v_c[...] if n == 0 else v_t[:, r0 - ATTN_BLOCK:r0]
            v_cat = jnp.concatenate([v_prev, v_t[:, r0:r1]], axis=1)
            o = _dot(v_cat, probs[n, hd])[hd * HEAD_DIM:(hd + 1) * HEAD_DIM] / denoms[n, hd]
            for g in range(Q_PER_KV):
                head_rows[Q_PER_KV * hd + g][n] = o[:, g * ATTN_BLOCK:(g + 1) * ATTN_BLOCK]
    attn_t = jnp.concatenate([jnp.concatenate(blocks, axis=1) for blocks in head_rows], axis=0)
    for hd in range(N_KV_HEADS):
        k_c[hd] = k_dup[hd][T - ATTN_BLOCK:]
    v_c[...] = v_t[:, T - ATTN_BLOCK:]
    gated = []
    for n in range(T // CHUNK):
        r0, r1 = n * CHUNK, (n + 1) * CHUNK
        s_cols = []
        for c in range(SGU_WIDTH // LANES):
            rhs = jnp.concatenate([v_lo[c][r0:r1], v_hi[c][r0:r1]], axis=0)
            s_cols.append(_dot(w_tril[:, 2 * c * CHUNK:2 * (c + 1) * CHUNK], rhs))
        gated.append(u_act[r0:r1] * (jnp.concatenate(s_cols, axis=1) + bs_ref[...]))

    attn = attn_t.T.astype(BF16)
    sgu = jnp.concatenate(gated, axis=0).astype(BF16)
    merged = g_a * y_a + g_b * _dot(attn, wpb_ref[...]) + g_c * _dot(sgu, wpc_ref[...])
    o_ref[0] = x + _dot(merged.astype(BF16), wout_ref[...])


def _mixer_layer(x, cos, sin, sinks, norm1, w_in, qn, kn, seg, wpool, pscale, wpa, wpb, wpc, vn,
                 ws, bs, wout):
    B, S, D = x.shape
    T = MIXER_TILE
    row_spec = lambda w: pl.BlockSpec((1, T, w), lambda b, j: (b, j, 0))
    consts = (norm1, w_in, qn, kn, seg, wpool, pscale, wpa, wpb, wpc, vn, ws, bs, wout)
    return pl.pallas_call(
        _mixer_kernel,
        grid=(B, S // T),
        in_specs=[pl.BlockSpec(memory_space=pltpu.SMEM), row_spec(D), row_spec(LANES),
                  row_spec(LANES)] + [_const_spec(c.shape) for c in consts],
        out_specs=row_spec(D),
        out_shape=jax.ShapeDtypeStruct(x.shape, x.dtype),
        scratch_shapes=[pltpu.VMEM((POOL_HALO, POOL_WIDTH), F32),
                        pltpu.VMEM((N_KV_HEADS, ATTN_BLOCK, LANES), BF16),
                        pltpu.VMEM((KV_WIDTH, ATTN_BLOCK), BF16)],
        compiler_params=pltpu.CompilerParams(
            dimension_semantics=("arbitrary", "arbitrary"), vmem_limit_bytes=VMEM_LIMIT),
        name="mixer",
    )(sinks, x, cos, sin, *consts)


def _ffn_kernel(x_ref, norm2_ref, wup_ref, cw_ref, cb_ref, wdown_ref, o_ref, tail, act_buf, stage):
    T = x_ref.shape[1]
    R = T // SUBLANES
    j = pl.program_id(1)

    @pl.when(j == 0)
    def _():
        tail[...] = jnp.zeros_like(tail)

    n_col = x_ref.shape[2] // LANES
    for cb in range(n_col):
        stage[cb] = x_ref[0, :, cb * LANES:(cb + 1) * LANES]
    x = jnp.concatenate(
        [jnp.concatenate([stage[cb, pl.ds(p, R, stride=SUBLANES), :] for cb in range(n_col)], axis=1)
         for p in range(SUBLANES)], axis=0)
    h = _rms_rows(x, norm2_ref[...]).astype(BF16)
    first_row = lax.broadcasted_iota(jnp.int32, (R, FFN_CHUNK), 0) == 0

    def up(c):
        lo = c * FFN_CHUNK
        return (_dot(h, wup_ref[:, lo:lo + FFN_CHUNK]),
                _dot(h, wup_ref[:, D_FF + lo:D_FF + lo + FFN_CHUNK]))

    def conv(u, lo, half, c):
        w = cw_ref[:, lo:lo + FFN_CHUNK]
        u6, u7 = u[6 * R:7 * R], u[7 * R:]
        s6 = jnp.where(first_row, tail[c, half, 0, SUBLANES - 1:, :], pltpu.roll(u6, 1, axis=0))
        s7 = jnp.where(first_row, tail[c, half, 1, SUBLANES - 1:, :], pltpu.roll(u7, 1, axis=0))
        tail[c, half, 0] = u6[R - SUBLANES:]
        tail[c, half, 1] = u7[R - SUBLANES:]
        prev1 = jnp.concatenate([s7, u[:7 * R]], axis=0)
        prev2 = jnp.concatenate([s6, s7, u[:6 * R]], axis=0)
        return u * w[2:3] + prev1 * w[1:2] + prev2 * w[0:1] + cb_ref[:, lo:lo + FFN_CHUNK]

    def act(c, u_gate, u_val):
        lo = c * FFN_CHUNK
        a = jax.nn.silu(conv(u_gate, lo, 0, c)) * conv(u_val, D_FF + lo, 1, c)
        act_buf[:, lo:lo + FFN_CHUNK] = a.astype(BF16)

    u_next = up(0)
    for c in range(FFN_CHUNKS):
        u_cur = u_next
        if c + 1 < FFN_CHUNKS:
            u_next = up(c + 1)
        act(c, *u_cur)
    out = x + _dot(act_buf[...], wdown_ref[...])
    for cb in range(n_col):
        for p in range(SUBLANES):
            stage[cb, pl.ds(p, R, stride=SUBLANES), :] = out[p * R:(p + 1) * R,
                                                             cb * LANES:(cb + 1) * LANES]
        o_ref[0, :, cb * LANES:(cb + 1) * LANES] = stage[cb]


def _ffn_layer(x, norm2, wup, cw, cb, wdown):
    B, S, D = x.shape
    T = FFN_TILE
    row_spec = pl.BlockSpec((1, T, D), lambda b, j: (b, j, 0))
    consts = (norm2, wup, cw, cb, wdown)
    return pl.pallas_call(
        _ffn_kernel,
        grid=(B, S // T),
        in_specs=[row_spec] + [_const_spec(c.shape) for c in consts],
        out_specs=row_spec,
        out_shape=jax.ShapeDtypeStruct(x.shape, x.dtype),
        scratch_shapes=[pltpu.VMEM((FFN_CHUNKS, 2, CONV_WIDTH - 1, SUBLANES, FFN_CHUNK), F32),
                        pltpu.VMEM((T, D_FF), BF16),
                        pltpu.VMEM((D // LANES, T, LANES), F32)],
        compiler_params=pltpu.CompilerParams(
            dimension_semantics=("arbitrary", "arbitrary"), vmem_limit_bytes=VMEM_LIMIT),
        name="ffn",
    )(x, *consts)


def _block_diag(blocks):
    G, a, b = blocks.shape
    eye = jnp.eye(G, dtype=blocks.dtype)
    return (eye[:, None, :, None] * blocks[:, :, None, :]).reshape(G * a, G * b)


def kernel(x, positions, norm1, w_in, q_norm, k_norm, sinks, w_pool, pool_scale, sgu_v_norm, w_s,
           b_s, w_proj_a, w_proj_b, w_proj_c, w_out, norm2, w_up, conv_w, conv_b, w_down):
    cos, sin = _rope_tables(positions)
    seg = _block_diag(jnp.ones((2 * LANES // HEAD_DIM, HEAD_DIM, HEAD_DIM), BF16))
    row = lambda a: a[None, :].astype(F32)
    for l in range(DEPTH):
        x = _mixer_layer(
            x, cos, sin, sinks[l], row(norm1[l]), w_in[l].astype(BF16),
            row(jnp.tile(q_norm[l], N_Q_HEADS)), row(jnp.tile(k_norm[l], N_KV_HEADS)), seg,
            _block_diag(w_pool[l]).astype(BF16), row(pool_scale[l]),
            w_proj_a[l].astype(BF16), w_proj_b[l].astype(BF16), w_proj_c[l].astype(BF16),
            row(jnp.tile(sgu_v_norm[l], SGU_WIDTH // POOL_GROUP_DIM)),
            jnp.transpose(w_s[l], (1, 0, 2)).reshape(CHUNK, -1).astype(BF16),
            jnp.repeat(b_s[l].T, POOL_GROUP_DIM, axis=1),
            w_out[l].astype(BF16))
        x = _ffn_layer(x, row(norm2[l]), w_up[l].astype(BF16), conv_w[l], row(conv_b[l]),
                       w_down[l].astype(BF16))
    return x
```

```python
import functools

import jax
import jax.numpy as jnp
from jax import lax
from jax.experimental import pallas as pl
from jax.experimental.pallas import tpu as pltpu

D_MODEL = 1024
DEPTH = 2
HEAD_DIM = 64
POOL_WINDOWS = (2, 4, 8, 16)
POOL_GROUP_DIM = 64
POOL_WIDTH = 256
N_Q_HEADS = 8
N_KV_HEADS = 2
Q_PER_KV = 4
ATTN_BLOCK = 128
ATTN_WIDTH = 512
KV_WIDTH = 128
CHUNK = 128
SGU_WIDTH = 256
N_BRANCHES = 3
D_FF = 2816
CONV_WIDTH = 3
ROPE_THETA = 10000.0
EPS = 1e-6

_C_POOL = 0
_C_Q = _C_POOL + POOL_WIDTH
_C_K = _C_Q + ATTN_WIDTH
_C_V = _C_K + KV_WIDTH
_C_SU = _C_V + KV_WIDTH
_C_SV = _C_SU + SGU_WIDTH
_C_GATE = _C_SV + SGU_WIDTH
IN_COLS = _C_GATE + N_BRANCHES * D_MODEL

LANES = 128
SUBLANES = 8
POOL_HALO = 16
MIXER_TILE = 512
FFN_TILE = 512
FFN_CHUNK = 256
FFN_CHUNKS = D_FF // FFN_CHUNK
ROPE_TILE = 2048
W_IN_STAGE_ROWS = 128
W_UP_STAGE_ROWS = 128
PROJ_STAGE_ROWS = 256
VMEM_LIMIT = 56 * 1024 * 1024

F32 = jnp.float32
BF16 = jnp.bfloat16


def _dot(a, b):
    return jnp.dot(a, b, preferred_element_type=F32)


def _const_spec(shape):
    zeros = (0,) * len(shape)
    return pl.BlockSpec(shape, lambda *_: zeros, pipeline_mode=pl.Buffered(1))


def _rms_rows(x, gain):
    ms = jnp.mean(x * x, axis=-1, keepdims=True)
    return x * lax.rsqrt(ms + EPS) * gain


def _load_weights_bf16(jobs, stage, sem):
    def copy(i):
        rows, width = jobs[i][0].shape
        return pltpu.make_async_copy(jobs[i][0], stage.at[i % 2, :rows, :width], sem.at[i % 2])

    copy(0).start()
    for i, (src, dst) in enumerate(jobs):
        if i + 1 < len(jobs):
            copy(i + 1).start()
        copy(i).wait()
        rows, width = src.shape
        dst[...] = stage[i % 2, :rows, :width].astype(BF16)


def _row_chunks(src, dst, rows):
    total = dst.shape[0]
    assert total % rows == 0
    return [(src.at[pl.ds(r, rows), :], dst.at[pl.ds(r, rows), :]) for r in range(0, total, rows)]


def _first_step():
    return (pl.program_id(0) == 0) & (pl.program_id(1) == 0)


def _swap_halves(x):
    lane = lax.broadcasted_iota(jnp.int32, x.shape, 1)
    fwd = pltpu.roll(x, LANES - HEAD_DIM // 2, axis=1)
    bwd = pltpu.roll(x, HEAD_DIM // 2, axis=1)
    return jnp.where((lane & (HEAD_DIM // 2)) == 0, fwd, bwd)


def _dup_heads(x):
    lane = lax.broadcasted_iota(jnp.int32, x.shape, 1)
    rolled = pltpu.roll(x, HEAD_DIM, axis=1)
    lo = lane < HEAD_DIM
    return jnp.where(lo, x, rolled), jnp.where(lo, rolled, x)


def _rope_kernel(pos_ref, freq_ref, cos_ref, sin_ref):
    ang = pos_ref[0].astype(F32) * freq_ref[...]
    lane = lax.broadcasted_iota(jnp.int32, ang.shape, 1)
    sign = jnp.where((lane & (HEAD_DIM // 2)) == 0, -1.0, 1.0)
    cos_ref[0] = jnp.cos(ang)
    sin_ref[0] = jnp.sin(ang) * sign


def _rope_tables(positions):
    B, S = positions.shape
    inv_freq = ROPE_THETA ** (-jnp.arange(0, HEAD_DIM, 2, dtype=F32) / HEAD_DIM)
    freq = jnp.tile(inv_freq, LANES // (HEAD_DIM // 2))[None, :]
    out = jax.ShapeDtypeStruct((B, S, LANES), F32)
    return pl.pallas_call(
        _rope_kernel,
        grid=(B, S // ROPE_TILE),
        in_specs=[pl.BlockSpec((1, ROPE_TILE, 1), lambda b, j: (b, j, 0)),
                  pl.BlockSpec((1, LANES), lambda b, j: (0, 0))],
        out_specs=[pl.BlockSpec((1, ROPE_TILE, LANES), lambda b, j: (b, j, 0))] * 2,
        out_shape=[out, out],
        name="rope_tables",
    )(positions.reshape(B, S, 1), freq)


def _mixer_kernel(sinks_ref, x_ref, cos_ref, sin_ref, norm1_ref, w_in_hbm, qn_ref, kn_ref,
                  seg_ref, wpool_ref, pscale_ref, wpa_hbm, wpb_hbm, wpc_hbm, vn_ref, ws_ref,
                  bs_ref, wout_hbm, o_ref, pool_c, k_c, v_c, w_in_ref, wpa_ref, wpb_ref, wpc_ref,
                  wout_ref, stage_in, stage_proj, sem_in, sem_proj, *, layer):
    T = x_ref.shape[1]
    n_blocks = T // ATTN_BLOCK
    j = pl.program_id(1)

    @pl.when(_first_step())
    def _():
        _load_weights_bf16(_row_chunks(w_in_hbm.at[layer], w_in_ref, W_IN_STAGE_ROWS),
                           stage_in, sem_in)
        _load_weights_bf16(
            _row_chunks(wpa_hbm.at[layer], wpa_ref, PROJ_STAGE_ROWS)
            + _row_chunks(wpb_hbm.at[layer], wpb_ref, PROJ_STAGE_ROWS)
            + _row_chunks(wpc_hbm.at[layer], wpc_ref, PROJ_STAGE_ROWS)
            + _row_chunks(wout_hbm.at[layer], wout_ref, PROJ_STAGE_ROWS), stage_proj, sem_proj)

    @pl.when(j == 0)
    def _():
        pool_c[...] = jnp.zeros_like(pool_c)
        k_c[...] = jnp.zeros_like(k_c)
        v_c[...] = jnp.zeros_like(v_c)

    def gate(i):
        lo = _C_GATE + i * D_MODEL
        return jax.nn.sigmoid(_dot(h, w_in_ref[:, lo:lo + D_MODEL]))

    def cols_of(t):
        return [t[:, c * LANES:(c + 1) * LANES] for c in range(t.shape[1] // LANES)]

    def split_bf16(t):
        hi = t.astype(BF16)
        return hi, (t - hi.astype(F32)).astype(BF16)

    x = x_ref[0]
    h = _rms_rows(x, norm1_ref[...]).astype(BF16)
    z = _dot(h, w_in_ref[:, :_C_GATE])
    xp = z[:, _C_POOL:_C_POOL + POOL_WIDTH]
    q = z[:, _C_Q:_C_Q + ATTN_WIDTH]
    k = z[:, _C_K:_C_K + KV_WIDTH]
    v = z[:, _C_V:_C_V + KV_WIDTH]
    su = z[:, _C_SU:_C_SU + SGU_WIDTH]
    sv = z[:, _C_SV:_C_SV + SGU_WIDTH]
    g_a = gate(0)

    ext = jnp.concatenate([pool_c[...], xp], axis=0)
    pool_c[...] = xp[T - POOL_HALO:, :]
    e1 = ext + pltpu.roll(ext, 1, axis=0)
    e2 = e1 + pltpu.roll(e1, 2, axis=0)
    e3 = e2 + pltpu.roll(e2, 4, axis=0)
    e4 = e3 + pltpu.roll(e3, 8, axis=0)
    lane = lax.broadcasted_iota(jnp.int32, (T, POOL_WIDTH), 1)
    row = lax.broadcasted_iota(jnp.int32, (T, POOL_WIDTH), 0)
    g1, g2, g3 = lane < POOL_GROUP_DIM, lane < 2 * POOL_GROUP_DIM, lane < 3 * POOL_GROUP_DIM
    wsum = jnp.where(g1, e1[POOL_HALO:], jnp.where(g2, e2[POOL_HALO:],
                     jnp.where(g3, e3[POOL_HALO:], e4[POOL_HALO:])))
    win = jnp.where(g1, POOL_WINDOWS[0], jnp.where(g2, POOL_WINDOWS[1],
                    jnp.where(g3, POOL_WINDOWS[2], POOL_WINDOWS[3])))
    count = jnp.minimum(j * T + row + 1, win).astype(F32)
    diff = (wsum / count - xp).astype(BF16)
    u_act = jax.nn.gelu(su)
    v_act = jax.nn.gelu(sv)
    sq_parts = [split_bf16(t * t) for t in (q[:, :2 * LANES], q[:, 2 * LANES:], v_act)]
    kk_parts = split_bf16(k * k)

    seg = seg_ref[...]
    ms_q0, ms_q1, ms_v = [(_dot(hi, seg) + _dot(lo, seg)) * (1.0 / HEAD_DIM) for hi, lo in sq_parts]
    ms_k = (_dot(kk_parts[0], seg_ref[:LANES, :LANES])
            + _dot(kk_parts[1], seg_ref[:LANES, :LANES])) * (1.0 / HEAD_DIM)
    mixed = (_dot(diff, wpool_ref[...]) * pscale_ref[...]).astype(BF16)
    g_b = gate(1)

    cos = cos_ref[0]
    sin = sin_ref[0]
    ms_q = jnp.concatenate([ms_q0, ms_q1], axis=1)
    q_n = q * lax.rsqrt(ms_q + EPS) * (qn_ref[...] * (HEAD_DIM ** -0.5))
    k_n = k * lax.rsqrt(ms_k + EPS) * kn_ref[...]
    q_cols = [tc * cos + _swap_halves(tc) * sin for tc in cols_of(q_n)]
    k_rot = k_n * cos + _swap_halves(k_n) * sin
    lo1 = lax.broadcasted_iota(jnp.int32, (T, LANES), 1) < HEAD_DIM
    q_lo = [jnp.where(lo1, qc, 0.0).astype(BF16) for qc in q_cols]
    q_hi = [jnp.where(lo1, 0.0, qc).astype(BF16) for qc in q_cols]
    k_dup = [t.astype(BF16) for t in _dup_heads(k_rot)]
    v_n = v_act * lax.rsqrt(ms_v + EPS) * vn_ref[...]
    v_lo = [jnp.where(lo1, vc, 0.0).astype(BF16) for vc in cols_of(v_n)]
    v_hi = [jnp.where(lo1, 0.0, vc).astype(BF16) for vc in cols_of(v_n)]
    wt = lax.broadcasted_iota(jnp.int32, ws_ref.shape, 0)
    wsrc = lax.broadcasted_iota(jnp.int32, ws_ref.shape, 1) & (CHUNK - 1)
    w_tril = jnp.where(wsrc <= wt, ws_ref[...], jnp.zeros_like(ws_ref))

    y_a = _dot(mixed, wpa_ref[...])
    scores = {}
    for n in range(n_blocks):
        r0, r1 = n * ATTN_BLOCK, (n + 1) * ATTN_BLOCK
        for hd in range(N_KV_HEADS):
            k_prev = k_c[hd] if n == 0 else k_dup[hd][r0 - ATTN_BLOCK:r0]
            k_cat = jnp.concatenate([k_prev, k_dup[hd][r0:r1]], axis=0)
            c0, c1 = 2 * hd, 2 * hd + 1
            q_all = jnp.concatenate([q_lo[c0][r0:r1], q_hi[c0][r0:r1],
                                     q_lo[c1][r0:r1], q_hi[c1][r0:r1]], axis=0)
            scores[n, hd] = lax.dot_general(k_cat, q_all, (((1,), (1,)), ((), ())),
                                            preferred_element_type=F32)
    g_c = gate(2)

    kj = lax.broadcasted_iota(jnp.int32, (2 * ATTN_BLOCK, ATTN_BLOCK), 0)
    qi = lax.broadcasted_iota(jnp.int32, (2 * ATTN_BLOCK, ATTN_BLOCK), 1)
    band = (kj > qi) & (kj <= qi + ATTN_BLOCK)
    first_band = band & (kj >= jnp.where(j == 0, ATTN_BLOCK, 0))
    probs, denoms = {}, {}
    for (n, hd), s in scores.items():
        mask = jnp.concatenate([first_band if n == 0 else band] * Q_PER_KV, axis=1)
        s = jnp.where(mask, s, -jnp.inf)
        sink = jnp.concatenate(
            [jnp.full((1, ATTN_BLOCK), sinks_ref[Q_PER_KV * hd + g], F32)
             for g in range(Q_PER_KV)], axis=1)
        m = jnp.maximum(jnp.max(s, axis=0, keepdims=True), sink)
        p = jnp.exp(s - m)
        denoms[n, hd] = jnp.sum(p, axis=0, keepdims=True) + jnp.exp(sink - m)
        probs[n, hd] = p.astype(BF16)

    v_t = v.T.astype(BF16)
    head_rows = [[None] * n_blocks for _ in range(N_Q_HEADS)]
    for n in range(n_blocks):
        r0, r1 = n * ATTN_BLOCK, (n + 1) * ATTN_BLOCK
        for hd in range(N_KV_HEADS):
            v_prev = v_c[...] if n == 0 else v_t[:, r0 - ATTN_BLOCK:r0]
            v_cat = jnp.concatenate([v_prev, v_t[:, r0:r1]], axis=1)
            o = _dot(v_cat, probs[n, hd])[hd * HEAD_DIM:(hd + 1) * HEAD_DIM] / denoms[n, hd]
            for g in range(Q_PER_KV):
                head_rows[Q_PER_KV * hd + g][n] = o[:, g * ATTN_BLOCK:(g + 1) * ATTN_BLOCK]
    attn_t = jnp.concatenate([jnp.concatenate(blocks, axis=1) for blocks in head_rows], axis=0)
    for hd in range(N_KV_HEADS):
        k_c[hd] = k_dup[hd][T - ATTN_BLOCK:]
    v_c[...] = v_t[:, T - ATTN_BLOCK:]
    gated = []
    for n in range(T // CHUNK):
        r0, r1 = n * CHUNK, (n + 1) * CHUNK
        s_cols = []
        for c in range(SGU_WIDTH // LANES):
            rhs = jnp.concatenate([v_lo[c][r0:r1], v_hi[c][r0:r1]], axis=0)
            s_cols.append(_dot(w_tril[:, 2 * c * CHUNK:2 * (c + 1) * CHUNK], rhs))
        gated.append(u_act[r0:r1] * (jnp.concatenate(s_cols, axis=1) + bs_ref[...]))

    attn = attn_t.T.astype(BF16)
    sgu = jnp.concatenate(gated, axis=0).astype(BF16)
    merged = g_a * y_a + g_b * _dot(attn, wpb_ref[...]) + g_c * _dot(sgu, wpc_ref[...])
    o_ref[0] = x + _dot(merged.astype(BF16), wout_ref[...])


def _mixer_layer(layer, x, cos, sin, sinks, norm1, w_in, qn, kn, seg, wpool, pscale, wpa, wpb, wpc,
                 vn, ws, bs, wout):
    B, S, D = x.shape
    T = MIXER_TILE
    row_spec = lambda w: pl.BlockSpec((1, T, w), lambda b, j: (b, j, 0))
    hbm = pl.BlockSpec(memory_space=pl.ANY)
    vmem = lambda a: _const_spec(a.shape)
    return pl.pallas_call(
        functools.partial(_mixer_kernel, layer=layer),
        grid=(B, S // T),
        in_specs=[pl.BlockSpec(memory_space=pltpu.SMEM), row_spec(D), row_spec(LANES),
                  row_spec(LANES), vmem(norm1), hbm, vmem(qn), vmem(kn), vmem(seg), vmem(wpool),
                  vmem(pscale), hbm, hbm, hbm, vmem(vn), vmem(ws), vmem(bs), hbm],
        out_specs=row_spec(D),
        out_shape=jax.ShapeDtypeStruct(x.shape, x.dtype),
        scratch_shapes=[pltpu.VMEM((POOL_HALO, POOL_WIDTH), F32),
                        pltpu.VMEM((N_KV_HEADS, ATTN_BLOCK, LANES), BF16),
                        pltpu.VMEM((KV_WIDTH, ATTN_BLOCK), BF16),
                        pltpu.VMEM(w_in.shape[1:], BF16), pltpu.VMEM(wpa.shape[1:], BF16),
                        pltpu.VMEM(wpb.shape[1:], BF16), pltpu.VMEM(wpc.shape[1:], BF16),
                        pltpu.VMEM(wout.shape[1:], BF16),
                        pltpu.VMEM((2, W_IN_STAGE_ROWS, IN_COLS), F32),
                        pltpu.VMEM((2, PROJ_STAGE_ROWS, D_MODEL), F32),
                        pltpu.SemaphoreType.DMA((2,)), pltpu.SemaphoreType.DMA((2,))],
        compiler_params=pltpu.CompilerParams(
            dimension_semantics=("arbitrary", "arbitrary"), vmem_limit_bytes=VMEM_LIMIT),
        name="mixer",
    )(sinks, x, cos, sin, norm1, w_in, qn, kn, seg, wpool, pscale, wpa, wpb, wpc, vn, ws, bs, wout)


def _ffn_kernel(x_ref, norm2_ref, wup_hbm, cw_ref, cb_ref, wdown_hbm, o_ref, tail, act_buf, stage,
                wup_ref, wdown_ref, stage_up, stage_down, sem_up, sem_down, *, layer):
    T = x_ref.shape[1]
    R = T // SUBLANES
    j = pl.program_id(1)

    @pl.when(_first_step())
    def _():
        _load_weights_bf16(_row_chunks(wup_hbm.at[layer], wup_ref, W_UP_STAGE_ROWS),
                           stage_up, sem_up)
        _load_weights_bf16(_row_chunks(wdown_hbm.at[layer], wdown_ref, PROJ_STAGE_ROWS),
                           stage_down, sem_down)

    @pl.when(j == 0)
    def _():
        tail[...] = jnp.zeros_like(tail)

    n_col = x_ref.shape[2] // LANES
    for cb in range(n_col):
        stage[cb] = x_ref[0, :, cb * LANES:(cb + 1) * LANES]
    x = jnp.concatenate(
        [jnp.concatenate([stage[cb, pl.ds(p, R, stride=SUBLANES), :] for cb in range(n_col)], axis=1)
         for p in range(SUBLANES)], axis=0)
    h = _rms_rows(x, norm2_ref[...]).astype(BF16)
    first_row = lax.broadcasted_iota(jnp.int32, (R, FFN_CHUNK), 0) == 0

    def up(c):
        lo = c * FFN_CHUNK
        return (_dot(h, wup_ref[:, lo:lo + FFN_CHUNK]),
                _dot(h, wup_ref[:, D_FF + lo:D_FF + lo + FFN_CHUNK]))

    def conv(u, lo, half, c):
        w = cw_ref[:, lo:lo + FFN_CHUNK]
        u6, u7 = u[6 * R:7 * R], u[7 * R:]
        s6 = jnp.where(first_row, tail[c, half, 0, SUBLANES - 1:, :], pltpu.roll(u6, 1, axis=0))
        s7 = jnp.where(first_row, tail[c, half, 1, SUBLANES - 1:, :], pltpu.roll(u7, 1, axis=0))
        tail[c, half, 0] = u6[R - SUBLANES:]
        tail[c, half, 1] = u7[R - SUBLANES:]
        prev1 = jnp.concatenate([s7, u[:7 * R]], axis=0)
        prev2 = jnp.concatenate([s6, s7, u[:6 * R]], axis=0)
        return u * w[2:3] + prev1 * w[1:2] + prev2 * w[0:1] + cb_ref[:, lo:lo + FFN_CHUNK]

    def act(c, u_gate, u_val):
        lo = c * FFN_CHUNK
        a = jax.nn.silu(conv(u_gate, lo, 0, c)) * conv(u_val, D_FF + lo, 1, c)
        act_buf[:, lo:lo + FFN_CHUNK] = a.astype(BF16)

    u_next = up(0)
    for c in range(FFN_CHUNKS):
        u_cur = u_next
        if c + 1 < FFN_CHUNKS:
            u_next = up(c + 1)
        act(c, *u_cur)
    out = x + _dot(act_buf[...], wdown_ref[...])
    for cb in range(n_col):
        for p in range(SUBLANES):
            stage[cb, pl.ds(p, R, stride=SUBLANES), :] = out[p * R:(p + 1) * R,
                                                             cb * LANES:(cb + 1) * LANES]
        o_ref[0, :, cb * LANES:(cb + 1) * LANES] = stage[cb]


def _ffn_layer(layer, x, norm2, wup, cw, cb, wdown):
    B, S, D = x.shape
    T = FFN_TILE
    row_spec = pl.BlockSpec((1, T, D), lambda b, j: (b, j, 0))
    hbm = pl.BlockSpec(memory_space=pl.ANY)
    return pl.pallas_call(
        functools.partial(_ffn_kernel, layer=layer),
        grid=(B, S // T),
        in_specs=[row_spec, _const_spec(norm2.shape), hbm, _const_spec(cw.shape),
                  _const_spec(cb.shape), hbm],
        out_specs=row_spec,
        out_shape=jax.ShapeDtypeStruct(x.shape, x.dtype),
        scratch_shapes=[pltpu.VMEM((FFN_CHUNKS, 2, CONV_WIDTH - 1, SUBLANES, FFN_CHUNK), F32),
                        pltpu.VMEM((T, D_FF), BF16),
                        pltpu.VMEM((D // LANES, T, LANES), F32),
                        pltpu.VMEM(wup.shape[1:], BF16), pltpu.VMEM(wdown.shape[1:], BF16),
                        pltpu.VMEM((2, W_UP_STAGE_ROWS, 2 * D_FF), F32),
                        pltpu.VMEM((2, PROJ_STAGE_ROWS, D_MODEL), F32),
                        pltpu.SemaphoreType.DMA((2,)), pltpu.SemaphoreType.DMA((2,))],
        compiler_params=pltpu.CompilerParams(
            dimension_semantics=("arbitrary", "arbitrary"), vmem_limit_bytes=VMEM_LIMIT),
        name="ffn",
    )(x, norm2, wup, cw, cb, wdown)


def _block_diag(blocks):
    G, a, b = blocks.shape
    eye = jnp.eye(G, dtype=blocks.dtype)
    return (eye[:, None, :, None] * blocks[:, :, None, :]).reshape(G * a, G * b)


def kernel(x, positions, norm1, w_in, q_norm, k_norm, sinks, w_pool, pool_scale, sgu_v_norm, w_s,
           b_s, w_proj_a, w_proj_b, w_proj_c, w_out, norm2, w_up, conv_w, conv_b, w_down):
    cos, sin = _rope_tables(positions)
    seg = _block_diag(jnp.ones((2 * LANES // HEAD_DIM, HEAD_DIM, HEAD_DIM), BF16))
    row = lambda a: a[None, :].astype(F32)
    for l in range(DEPTH):
        x = _mixer_layer(
            l, x, cos, sin, sinks[l], row(norm1[l]), w_in,
            row(jnp.tile(q_norm[l], N_Q_HEADS)), row(jnp.tile(k_norm[l], N_KV_HEADS)), seg,
            _block_diag(w_pool[l]).astype(BF16), row(pool_scale[l]),
            w_proj_a, w_proj_b, w_proj_c,
            row(jnp.tile(sgu_v_norm[l], SGU_WIDTH // POOL_GROUP_DIM)),
            jnp.transpose(w_s[l], (1, 0, 2)).reshape(CHUNK, -1).astype(BF16),
            jnp.repeat(b_s[l].T, POOL_GROUP_DIM, axis=1),
            w_out)
        x = _ffn_layer(l, x, row(norm2[l]), w_up, conv_w[l], row(conv_b[l]), w_down)
    return x
```

```python
import functools

import jax
import jax.numpy as jnp
from jax import lax
from jax.experimental import pallas as pl
from jax.experimental.pallas import tpu as pltpu

D_MODEL = 1024
DEPTH = 2
HEAD_DIM = 64
POOL_WINDOWS = (2, 4, 8, 16)
POOL_GROUP_DIM = 64
POOL_WIDTH = 256
N_Q_HEADS = 8
N_KV_HEADS = 2
Q_PER_KV = 4
ATTN_BLOCK = 128
ATTN_WIDTH = 512
KV_WIDTH = 128
CHUNK = 128
SGU_WIDTH = 256
N_BRANCHES = 3
D_FF = 2816
CONV_WIDTH = 3
ROPE_THETA = 10000.0
EPS = 1e-6

_C_POOL = 0
_C_Q = _C_POOL + POOL_WIDTH
_C_K = _C_Q + ATTN_WIDTH
_C_V = _C_K + KV_WIDTH
_C_SU = _C_V + KV_WIDTH
_C_SV = _C_SU + SGU_WIDTH
_C_GATE = _C_SV + SGU_WIDTH
IN_COLS = _C_GATE + N_BRANCHES * D_MODEL

LANES = 128
SUBLANES = 8
POOL_HALO = 16
MIXER_TILE = 512
FFN_TILE = 512
FFN_CHUNK = 256
FFN_CHUNKS = D_FF // FFN_CHUNK
W_IN_STAGE_ROWS = 128
W_UP_STAGE_ROWS = 128
PROJ_STAGE_ROWS = 256
VMEM_LIMIT = 56 * 1024 * 1024

F32 = jnp.float32
BF16 = jnp.bfloat16


def _dot(a, b):
    return jnp.dot(a, b, preferred_element_type=F32)


def _const_spec(shape):
    zeros = (0,) * len(shape)
    return pl.BlockSpec(shape, lambda *_: zeros, pipeline_mode=pl.Buffered(1))


def _rms_rows(x, gain):
    ms = jnp.mean(x * x, axis=-1, keepdims=True)
    return x * lax.rsqrt(ms + EPS) * gain


def _load_weights_bf16(jobs, stage, sem):
    def copy(i):
        rows, width = jobs[i][0].shape
        return pltpu.make_async_copy(jobs[i][0], stage.at[i % 2, :rows, :width], sem.at[i % 2])

    copy(0).start()
    for i, (src, dst) in enumerate(jobs):
        if i + 1 < len(jobs):
            copy(i + 1).start()
        copy(i).wait()
        rows, width = src.shape
        dst[...] = stage[i % 2, :rows, :width].astype(BF16)


def _row_chunks(src, dst, rows):
    total = dst.shape[0]
    assert total % rows == 0
    return [(src.at[pl.ds(r, rows), :], dst.at[pl.ds(r, rows), :]) for r in range(0, total, rows)]


def _first_step():
    return (pl.program_id(0) == 0) & (pl.program_id(1) == 0)


def _sigmoid(x):
    return 0.5 * jnp.tanh(0.5 * x) + 0.5


def _swap_halves(x):
    lane = lax.broadcasted_iota(jnp.int32, x.shape, 1)
    fwd = pltpu.roll(x, LANES - HEAD_DIM // 2, axis=1)
    bwd = pltpu.roll(x, HEAD_DIM // 2, axis=1)
    return jnp.where((lane & (HEAD_DIM // 2)) == 0, fwd, bwd)


def _dup_heads(x):
    lane = lax.broadcasted_iota(jnp.int32, x.shape, 1)
    rolled = pltpu.roll(x, HEAD_DIM, axis=1)
    lo = lane < HEAD_DIM
    return jnp.where(lo, x, rolled), jnp.where(lo, rolled, x)


def _mixer_kernel(sinks_ref, x_ref, rope_a, rope_b, norm1_ref, w_in_hbm, qn_ref, kn_ref,
                  seg_ref, wpool_ref, pscale_ref, wpa_hbm, wpb_hbm, wpc_hbm, vn_ref, ws_ref,
                  bs_ref, wout_hbm, *outs_and_scratch, layer, make_tables):
    n_out = 3 if make_tables else 1
    o_ref = outs_and_scratch[0]
    (pool_c, k_c, v_c, w_in_ref, wpa_ref, wpb_ref, wpc_ref, wout_ref, stage_in, stage_proj,
     sem_in, sem_proj) = outs_and_scratch[n_out:]
    T = x_ref.shape[1]
    n_blocks = T // ATTN_BLOCK
    j = pl.program_id(1)

    @pl.when(_first_step())
    def _():
        _load_weights_bf16(_row_chunks(w_in_hbm.at[layer], w_in_ref, W_IN_STAGE_ROWS),
                           stage_in, sem_in)
        _load_weights_bf16(
            _row_chunks(wpa_hbm.at[layer], wpa_ref, PROJ_STAGE_ROWS)
            + _row_chunks(wpb_hbm.at[layer], wpb_ref, PROJ_STAGE_ROWS)
            + _row_chunks(wpc_hbm.at[layer], wpc_ref, PROJ_STAGE_ROWS)
            + _row_chunks(wout_hbm.at[layer], wout_ref, PROJ_STAGE_ROWS), stage_proj, sem_proj)

    @pl.when(j == 0)
    def _():
        pool_c[...] = jnp.zeros_like(pool_c)
        k_c[...] = jnp.zeros_like(k_c)
        v_c[...] = jnp.zeros_like(v_c)

    def gate(i):
        lo = _C_GATE + i * D_MODEL
        return _sigmoid(_dot(h, w_in_ref[:, lo:lo + D_MODEL]))

    def cols_of(t):
        return [t[:, c * LANES:(c + 1) * LANES] for c in range(t.shape[1] // LANES)]

    def split_bf16(t):
        hi = t.astype(BF16)
        return hi, (t - hi.astype(F32)).astype(BF16)

    x = x_ref[0]
    h = _rms_rows(x, norm1_ref[...]).astype(BF16)
    z = _dot(h, w_in_ref[:, :_C_GATE])
    xp = z[:, _C_POOL:_C_POOL + POOL_WIDTH]
    q = z[:, _C_Q:_C_Q + ATTN_WIDTH]
    k = z[:, _C_K:_C_K + KV_WIDTH]
    v = z[:, _C_V:_C_V + KV_WIDTH]
    su = z[:, _C_SU:_C_SU + SGU_WIDTH]
    sv = z[:, _C_SV:_C_SV + SGU_WIDTH]
    g_a = gate(0)

    ext = jnp.concatenate([pool_c[...], xp], axis=0)
    pool_c[...] = xp[T - POOL_HALO:, :]
    e1 = ext + pltpu.roll(ext, 1, axis=0)
    e2 = e1 + pltpu.roll(e1, 2, axis=0)
    e3 = e2 + pltpu.roll(e2, 4, axis=0)
    e4 = e3 + pltpu.roll(e3, 8, axis=0)
    lane = lax.broadcasted_iota(jnp.int32, (T, POOL_WIDTH), 1)
    row = lax.broadcasted_iota(jnp.int32, (T, POOL_WIDTH), 0)
    g1, g2, g3 = lane < POOL_GROUP_DIM, lane < 2 * POOL_GROUP_DIM, lane < 3 * POOL_GROUP_DIM
    wsum = jnp.where(g1, e1[POOL_HALO:], jnp.where(g2, e2[POOL_HALO:],
                     jnp.where(g3, e3[POOL_HALO:], e4[POOL_HALO:])))
    win = jnp.where(g1, POOL_WINDOWS[0], jnp.where(g2, POOL_WINDOWS[1],
                    jnp.where(g3, POOL_WINDOWS[2], POOL_WINDOWS[3])))
    count = jnp.minimum(j * T + row + 1, win).astype(F32)
    diff = (wsum / count - xp).astype(BF16)
    u_act = jax.nn.gelu(su)
    v_act = jax.nn.gelu(sv)
    sq_parts = [split_bf16(t * t) for t in (q[:, :2 * LANES], q[:, 2 * LANES:], v_act)]
    kk_parts = split_bf16(k * k)

    seg = seg_ref[...]
    ms_q0, ms_q1, ms_v = [(_dot(hi, seg) + _dot(lo, seg)) * (1.0 / HEAD_DIM) for hi, lo in sq_parts]
    ms_k = (_dot(kk_parts[0], seg_ref[:LANES, :LANES])
            + _dot(kk_parts[1], seg_ref[:LANES, :LANES])) * (1.0 / HEAD_DIM)
    mixed = (_dot(diff, wpool_ref[...]) * pscale_ref[...]).astype(BF16)
    g_b = gate(1)

    if make_tables:
        cos_out, sin_out = outs_and_scratch[1:3]
        ang = rope_a[0].astype(F32) * rope_b[...]
        lane_r = lax.broadcasted_iota(jnp.int32, ang.shape, 1)
        cos = jnp.cos(ang)
        sin = jnp.sin(ang) * jnp.where((lane_r & (HEAD_DIM // 2)) == 0, -1.0, 1.0)
        cos_out[0] = cos
        sin_out[0] = sin
    else:
        cos = rope_a[0]
        sin = rope_b[0]
    ms_q = jnp.concatenate([ms_q0, ms_q1], axis=1)
    q_n = q * lax.rsqrt(ms_q + EPS) * (qn_ref[...] * (HEAD_DIM ** -0.5))
    k_n = k * lax.rsqrt(ms_k + EPS) * kn_ref[...]
    q_cols = [tc * cos + _swap_halves(tc) * sin for tc in cols_of(q_n)]
    k_rot = k_n * cos + _swap_halves(k_n) * sin
    lo1 = lax.broadcasted_iota(jnp.int32, (T, LANES), 1) < HEAD_DIM
    q_lo = [jnp.where(lo1, qc, 0.0).astype(BF16) for qc in q_cols]
    q_hi = [jnp.where(lo1, 0.0, qc).astype(BF16) for qc in q_cols]
    k_dup = [t.astype(BF16) for t in _dup_heads(k_rot)]
    v_n = v_act * lax.rsqrt(ms_v + EPS) * vn_ref[...]
    v_lo = [jnp.where(lo1, vc, 0.0).astype(BF16) for vc in cols_of(v_n)]
    v_hi = [jnp.where(lo1, 0.0, vc).astype(BF16) for vc in cols_of(v_n)]
    wt = lax.broadcasted_iota(jnp.int32, ws_ref.shape, 0)
    wsrc = lax.broadcasted_iota(jnp.int32, ws_ref.shape, 1) & (CHUNK - 1)
    w_tril = jnp.where(wsrc <= wt, ws_ref[...], jnp.zeros_like(ws_ref))

    y_a = _dot(mixed, wpa_ref[...])
    scores = {}
    for n in range(n_blocks):
        r0, r1 = n * ATTN_BLOCK, (n + 1) * ATTN_BLOCK
        for hd in range(N_KV_HEADS):
            k_prev = k_c[hd] if n == 0 else k_dup[hd][r0 - ATTN_BLOCK:r0]
            k_cat = jnp.concatenate([k_prev, k_dup[hd][r0:r1]], axis=0)
            c0, c1 = 2 * hd, 2 * hd + 1
            q_all = jnp.concatenate([q_lo[c0][r0:r1], q_hi[c0][r0:r1],
                                     q_lo[c1][r0:r1], q_hi[c1][r0:r1]], axis=0)
            scores[n, hd] = lax.dot_general(k_cat, q_all, (((1,), (1,)), ((), ())),
                                            preferred_element_type=F32)
    g_c = gate(2)

    kj = lax.broadcasted_iota(jnp.int32, (2 * ATTN_BLOCK, ATTN_BLOCK), 0)
    qi = lax.broadcasted_iota(jnp.int32, (2 * ATTN_BLOCK, ATTN_BLOCK), 1)
    band = (kj > qi) & (kj <= qi + ATTN_BLOCK)
    first_band = band & (kj >= jnp.where(j == 0, ATTN_BLOCK, 0))
    probs, denoms = {}, {}
    for (n, hd), s in scores.items():
        mask = jnp.concatenate([first_band if n == 0 else band] * Q_PER_KV, axis=1)
        s = jnp.where(mask, s, -jnp.inf)
        sink = jnp.concatenate(
            [jnp.full((1, ATTN_BLOCK), sinks_ref[Q_PER_KV * hd + g], F32)
             for g in range(Q_PER_KV)], axis=1)
        m = jnp.maximum(jnp.max(s, axis=0, keepdims=True), sink)
        p = jnp.exp(s - m)
        denoms[n, hd] = jnp.sum(p, axis=0, keepdims=True) + jnp.exp(sink - m)
        probs[n, hd] = p.astype(BF16)

    v_t = v.T.astype(BF16)
    head_rows = [[None] * n_blocks for _ in range(N_Q_HEADS)]
    for n in range(n_blocks):
        r0, r1 = n * ATTN_BLOCK, (n + 1) * ATTN_BLOCK
        for hd in range(N_KV_HEADS):
            v_prev = v_c[...] if n == 0 else v_t[:, r0 - ATTN_BLOCK:r0]
            v_cat = jnp.concatenate([v_prev, v_t[:, r0:r1]], axis=1)
            o = _dot(v_cat, probs[n, hd])[hd * HEAD_DIM:(hd + 1) * HEAD_DIM] / denoms[n, hd]
            for g in range(Q_PER_KV):
                head_rows[Q_PER_KV * hd + g][n] = o[:, g * ATTN_BLOCK:(g + 1) * ATTN_BLOCK]
    attn_t = jnp.concatenate([jnp.concatenate(blocks, axis=1) for blocks in head_rows], axis=0)
    for hd in range(N_KV_HEADS):
        k_c[hd] = k_dup[hd][T - ATTN_BLOCK:]
    v_c[...] = v_t[:, T - ATTN_BLOCK:]
    gated = []
    for n in range(T // CHUNK):
        r0, r1 = n * CHUNK, (n + 1) * CHUNK
        s_cols = []
        for c in range(SGU_WIDTH // LANES):
            rhs = jnp.concatenate([v_lo[c][r0:r1], v_hi[c][r0:r1]], axis=0)
            s_cols.append(_dot(w_tril[:, 2 * c * CHUNK:2 * (c + 1) * CHUNK], rhs))
        gated.append(u_act[r0:r1] * (jnp.concatenate(s_cols, axis=1) + bs_ref[...]))

    attn = attn_t.T.astype(BF16)
    sgu = jnp.concatenate(gated, axis=0).astype(BF16)
    merged = g_a * y_a + g_b * _dot(attn, wpb_ref[...]) + g_c * _dot(sgu, wpc_ref[...])
    o_ref[0] = x + _dot(merged.astype(BF16), wout_ref[...])


def _mixer_layer(layer, x, rope_a, rope_b, sinks, norm1, w_in, qn, kn, seg, wpool, pscale, wpa, wpb,
                 wpc, vn, ws, bs, wout, *, make_tables):
    B, S, D = x.shape
    T = MIXER_TILE
    row_spec = lambda w: pl.BlockSpec((1, T, w), lambda b, j: (b, j, 0))
    hbm = pl.BlockSpec(memory_space=pl.ANY)
    vmem = lambda a: _const_spec(a.shape)
    table = jax.ShapeDtypeStruct((B, S, LANES), F32)
    out_x = jax.ShapeDtypeStruct(x.shape, x.dtype)
    return pl.pallas_call(
        functools.partial(_mixer_kernel, layer=layer, make_tables=make_tables),
        grid=(B, S // T),
        in_specs=[pl.BlockSpec(memory_space=pltpu.SMEM), row_spec(D),
                  row_spec(1) if make_tables else row_spec(LANES),
                  vmem(rope_b) if make_tables else row_spec(LANES),
                  vmem(norm1), hbm, vmem(qn), vmem(kn), vmem(seg), vmem(wpool),
                  vmem(pscale), hbm, hbm, hbm, vmem(vn), vmem(ws), vmem(bs), hbm],
        out_specs=[row_spec(D), row_spec(LANES), row_spec(LANES)] if make_tables else row_spec(D),
        out_shape=[out_x, table, table] if make_tables else out_x,
        scratch_shapes=[pltpu.VMEM((POOL_HALO, POOL_WIDTH), F32),
                        pltpu.VMEM((N_KV_HEADS, ATTN_BLOCK, LANES), BF16),
                        pltpu.VMEM((KV_WIDTH, ATTN_BLOCK), BF16),
                        pltpu.VMEM(w_in.shape[1:], BF16), pltpu.VMEM(wpa.shape[1:], BF16),
                        pltpu.VMEM(wpb.shape[1:], BF16), pltpu.VMEM(wpc.shape[1:], BF16),
                        pltpu.VMEM(wout.shape[1:], BF16),
                        pltpu.VMEM((2, W_IN_STAGE_ROWS, IN_COLS), F32),
                        pltpu.VMEM((2, PROJ_STAGE_ROWS, D_MODEL), F32),
                        pltpu.SemaphoreType.DMA((2,)), pltpu.SemaphoreType.DMA((2,))],
        compiler_params=pltpu.CompilerParams(
            dimension_semantics=("arbitrary", "arbitrary"), vmem_limit_bytes=VMEM_LIMIT),
        name="mixer",
    )(sinks, x, rope_a, rope_b, norm1, w_in, qn, kn, seg, wpool, pscale, wpa, wpb, wpc, vn, ws, bs,
      wout)


def _ffn_kernel(*refs, layer):
    n_col = D_MODEL // LANES
    x_cols = refs[:n_col]
    (norm2_ref, wup_hbm, cw_ref, cb_ref, wdown_hbm, o_ref, tail, act_buf, stage,
     wup_ref, wdown_ref, stage_up, stage_down, sem_up, sem_down) = refs[n_col:]
    T = o_ref.shape[1]
    R = T // SUBLANES
    j = pl.program_id(1)

    @pl.when(_first_step())
    def _():
        _load_weights_bf16(_row_chunks(wup_hbm.at[layer], wup_ref, W_UP_STAGE_ROWS),
                           stage_up, sem_up)
        _load_weights_bf16(_row_chunks(wdown_hbm.at[layer], wdown_ref, PROJ_STAGE_ROWS),
                           stage_down, sem_down)

    @pl.when(j == 0)
    def _():
        tail[...] = jnp.zeros_like(tail)

    x = jnp.concatenate(
        [jnp.concatenate([xc[0, pl.ds(p, R, stride=SUBLANES), :] for xc in x_cols], axis=1)
         for p in range(SUBLANES)], axis=0)
    h = _rms_rows(x, norm2_ref[...]).astype(BF16)
    first_row = lax.broadcasted_iota(jnp.int32, (R, FFN_CHUNK), 0) == 0

    def up(c):
        lo = c * FFN_CHUNK
        return (_dot(h, wup_ref[:, lo:lo + FFN_CHUNK]),
                _dot(h, wup_ref[:, D_FF + lo:D_FF + lo + FFN_CHUNK]))

    def conv(u, lo, half, c):
        w = cw_ref[:, lo:lo + FFN_CHUNK]
        u6, u7 = u[6 * R:7 * R], u[7 * R:]
        s6 = jnp.where(first_row, tail[c, half, 0, SUBLANES - 1:, :], pltpu.roll(u6, 1, axis=0))
        s7 = jnp.where(first_row, tail[c, half, 1, SUBLANES - 1:, :], pltpu.roll(u7, 1, axis=0))
        tail[c, half, 0] = u6[R - SUBLANES:]
        tail[c, half, 1] = u7[R - SUBLANES:]
        prev1 = jnp.concatenate([s7, u[:7 * R]], axis=0)
        prev2 = jnp.concatenate([s6, s7, u[:6 * R]], axis=0)
        return u * w[2:3] + prev1 * w[1:2] + prev2 * w[0:1] + cb_ref[:, lo:lo + FFN_CHUNK]

    def act(c, u_gate, u_val):
        lo = c * FFN_CHUNK
        gate = conv(u_gate, lo, 0, c)
        a = gate * _sigmoid(gate) * conv(u_val, D_FF + lo, 1, c)
        act_buf[:, lo:lo + FFN_CHUNK] = a.astype(BF16)

    u_next = up(0)
    for c in range(FFN_CHUNKS):
        u_cur = u_next
        if c + 1 < FFN_CHUNKS:
            u_next = up(c + 1)
        act(c, *u_cur)
    out = x + _dot(act_buf[...], wdown_ref[...])
    for cb in range(n_col):
        for p in range(SUBLANES):
            stage[cb, pl.ds(p, R, stride=SUBLANES), :] = out[p * R:(p + 1) * R,
                                                             cb * LANES:(cb + 1) * LANES]
        o_ref[0, :, cb * LANES:(cb + 1) * LANES] = stage[cb]


def _ffn_layer(layer, x, norm2, wup, cw, cb, wdown):
    B, S, D = x.shape
    T = FFN_TILE
    row_spec = pl.BlockSpec((1, T, D), lambda b, j: (b, j, 0))
    hbm = pl.BlockSpec(memory_space=pl.ANY)
    return pl.pallas_call(
        functools.partial(_ffn_kernel, layer=layer),
        grid=(B, S // T),
        in_specs=[pl.BlockSpec((1, T, LANES), functools.partial(lambda b, j, c: (b, j, c), c=c))
                  for c in range(D // LANES)]
                 + [_const_spec(norm2.shape), hbm, _const_spec(cw.shape), _const_spec(cb.shape), hbm],
        out_specs=row_spec,
        out_shape=jax.ShapeDtypeStruct(x.shape, x.dtype),
        scratch_shapes=[pltpu.VMEM((FFN_CHUNKS, 2, CONV_WIDTH - 1, SUBLANES, FFN_CHUNK), F32),
                        pltpu.VMEM((T, D_FF), BF16),
                        pltpu.VMEM((D // LANES, T, LANES), F32),
                        pltpu.VMEM(wup.shape[1:], BF16), pltpu.VMEM(wdown.shape[1:], BF16),
                        pltpu.VMEM((2, W_UP_STAGE_ROWS, 2 * D_FF), F32),
                        pltpu.VMEM((2, PROJ_STAGE_ROWS, D_MODEL), F32),
                        pltpu.SemaphoreType.DMA((2,)), pltpu.SemaphoreType.DMA((2,))],
        compiler_params=pltpu.CompilerParams(
            dimension_semantics=("arbitrary", "arbitrary"), vmem_limit_bytes=VMEM_LIMIT),
        name="ffn",
    )(*([x] * (D // LANES)), norm2, wup, cw, cb, wdown)


def _block_diag(blocks):
    G, a, b = blocks.shape
    eye = jnp.eye(G, dtype=blocks.dtype)
    return (eye[:, None, :, None] * blocks[:, :, None, :]).reshape(G * a, G * b)


def kernel(x, positions, norm1, w_in, q_norm, k_norm, sinks, w_pool, pool_scale, sgu_v_norm, w_s,
           b_s, w_proj_a, w_proj_b, w_proj_c, w_out, norm2, w_up, conv_w, conv_b, w_down):
    B, S = positions.shape
    inv_freq = ROPE_THETA ** (-jnp.arange(0, HEAD_DIM, 2, dtype=F32) / HEAD_DIM)
    freq = jnp.tile(inv_freq, LANES // (HEAD_DIM // 2))[None, :]
    rope = (positions.reshape(B, S, 1), freq)
    seg = _block_diag(jnp.ones((2 * LANES // HEAD_DIM, HEAD_DIM, HEAD_DIM), BF16))
    row = lambda a: a[None, :].astype(F32)
    for l in range(DEPTH):
        out = _mixer_layer(
            l, x, *rope, sinks[l], row(norm1[l]), w_in,
            row(jnp.tile(q_norm[l], N_Q_HEADS)), row(jnp.tile(k_norm[l], N_KV_HEADS)), seg,
            _block_diag(w_pool[l]).astype(BF16), row(pool_scale[l]),
            w_proj_a, w_proj_b, w_proj_c,
            row(jnp.tile(sgu_v_norm[l], SGU_WIDTH // POOL_GROUP_DIM)),
            jnp.transpose(w_s[l], (1, 0, 2)).reshape(CHUNK, -1).astype(BF16),
            jnp.repeat(b_s[l].T, POOL_GROUP_DIM, axis=1),
            w_out, make_tables=(l == 0))
        if l == 0:
            x, *rope = out
        else:
            x = out
        x = _ffn_layer(l, x, row(norm2[l]), w_up, conv_w[l], row(conv_b[l]), w_down)
    return x
```

```python
import functools
import math

import jax
import jax.numpy as jnp
from jax import lax
from jax.experimental import pallas as pl
from jax.experimental.pallas import tpu as pltpu

D_MODEL = 1024
DEPTH = 2
HEAD_DIM = 64
POOL_WINDOWS = (2, 4, 8, 16)
POOL_GROUP_DIM = 64
POOL_WIDTH = 256
N_Q_HEADS = 8
N_KV_HEADS = 2
Q_PER_KV = 4
ATTN_BLOCK = 128
ATTN_WIDTH = 512
KV_WIDTH = 128
CHUNK = 128
SGU_WIDTH = 256
N_BRANCHES = 3
D_FF = 2816
CONV_WIDTH = 3
ROPE_THETA = 10000.0
EPS = 1e-6

_C_POOL = 0
_C_Q = _C_POOL + POOL_WIDTH
_C_K = _C_Q + ATTN_WIDTH
_C_V = _C_K + KV_WIDTH
_C_SU = _C_V + KV_WIDTH
_C_SV = _C_SU + SGU_WIDTH
_C_GATE = _C_SV + SGU_WIDTH
IN_COLS = _C_GATE + N_BRANCHES * D_MODEL

LANES = 128
SUBLANES = 8
POOL_HALO = 16
MIXER_TILE = 512
FFN_TILE = 512
FFN_CHUNK = 256
FFN_CHUNKS = D_FF // FFN_CHUNK
W_IN_STAGE_ROWS = 64
W_UP_STAGE_ROWS = 64
PROJ_STAGE_ROWS = 256
WEIGHT_STAGE_SLOTS = 3
VMEM_LIMIT = 56 * 1024 * 1024

F32 = jnp.float32
BF16 = jnp.bfloat16


def _dot(a, b):
    return jnp.dot(a, b, preferred_element_type=F32)


def _const_spec(shape):
    zeros = (0,) * len(shape)
    return pl.BlockSpec(shape, lambda *_: zeros, pipeline_mode=pl.Buffered(1))


def _rms_rows(x, gain):
    ms = jnp.mean(x * x, axis=-1, keepdims=True)
    return x * lax.rsqrt(ms + EPS) * gain


def _load_weights_bf16(jobs, stage, sem):
    slots = stage.shape[0]

    def copy(i):
        rows, width = jobs[i][0].shape
        return pltpu.make_async_copy(jobs[i][0], stage.at[i % slots, :rows, :width],
                                     sem.at[i % slots])

    for i in range(min(slots - 1, len(jobs))):
        copy(i).start()
    for i, (src, dst) in enumerate(jobs):
        if i + slots - 1 < len(jobs):
            copy(i + slots - 1).start()
        copy(i).wait()
        rows, width = src.shape
        dst[...] = stage[i % slots, :rows, :width].astype(BF16)


def _row_chunks(src, dst, rows):
    total = dst.shape[0]
    assert total % rows == 0
    return [(src.at[pl.ds(r, rows), :], dst.at[pl.ds(r, rows), :]) for r in range(0, total, rows)]


def _first_step():
    return (pl.program_id(0) == 0) & (pl.program_id(1) == 0)


def _sigmoid(x):
    return 0.5 * jnp.tanh(0.5 * x) + 0.5


def _split_half_pi(pieces=4, bits=8):
    rem, parts = math.pi / 2, []
    for _ in range(pieces - 1):
        mant, exp = math.frexp(rem)
        part = math.floor(mant * 2 ** bits) / 2 ** bits * 2 ** exp
        parts.append(part)
        rem -= part
    return parts + [rem]


_HALF_PI_PARTS = _split_half_pi()
_SIN_COEF = (-1.6666654611e-1, 8.3321608736e-3, -1.9515295891e-4)
_COS_COEF = (4.166664568298827e-2, -1.388731625493765e-3, 2.443315711809948e-5)


def _sincos(x):
    n = jnp.floor(x * (2.0 / math.pi) + 0.5)
    r = x
    for part in _HALF_PI_PARTS:
        r = r - n * part
    r2 = r * r
    s = r + r * r2 * (_SIN_COEF[0] + r2 * (_SIN_COEF[1] + r2 * _SIN_COEF[2]))
    c = 1.0 - 0.5 * r2 + r2 * r2 * (_COS_COEF[0] + r2 * (_COS_COEF[1] + r2 * _COS_COEF[2]))
    quad = n.astype(jnp.int32)
    odd = (quad & 1) == 1
    sin_x = jnp.where(odd, c, s)
    cos_x = jnp.where(odd, s, c)
    sin_x = jnp.where((quad & 2) == 0, sin_x, -sin_x)
    cos_x = jnp.where(((quad + 1) & 2) == 0, cos_x, -cos_x)
    return sin_x, cos_x


def _swap_halves(x):
    lane = lax.broadcasted_iota(jnp.int32, x.shape, 1)
    fwd = pltpu.roll(x, LANES - HEAD_DIM // 2, axis=1)
    bwd = pltpu.roll(x, HEAD_DIM // 2, axis=1)
    return jnp.where((lane & (HEAD_DIM // 2)) == 0, fwd, bwd)


def _dup_heads(x):
    lane = lax.broadcasted_iota(jnp.int32, x.shape, 1)
    rolled = pltpu.roll(x, HEAD_DIM, axis=1)
    lo = lane < HEAD_DIM
    return jnp.where(lo, x, rolled), jnp.where(lo, rolled, x)


def _mixer_kernel(sinks_ref, x_ref, rope_a, rope_b, norm1_ref, w_in_hbm, qn_ref, kn_ref,
                  seg_ref, wpool_ref, pscale_ref, wpa_hbm, wpb_hbm, wpc_hbm, vn_ref, ws_ref,
                  bs_ref, wout_hbm, *outs_and_scratch, layer, make_tables):
    n_out = 3 if make_tables else 1
    o_ref = outs_and_scratch[0]
    (pool_c, k_c, v_c, w_in_ref, wpa_ref, wpb_ref, wpc_ref, wout_ref, stage_in, stage_proj,
     sem_in, sem_proj) = outs_and_scratch[n_out:]
    T = x_ref.shape[1]
    n_blocks = T // ATTN_BLOCK
    j = pl.program_id(1)

    @pl.when(_first_step())
    def _():
        _load_weights_bf16(_row_chunks(w_in_hbm.at[layer], w_in_ref, W_IN_STAGE_ROWS),
                           stage_in, sem_in)
        _load_weights_bf16(
            _row_chunks(wpa_hbm.at[layer], wpa_ref, PROJ_STAGE_ROWS)
            + _row_chunks(wpb_hbm.at[layer], wpb_ref, PROJ_STAGE_ROWS)
            + _row_chunks(wpc_hbm.at[layer], wpc_ref, PROJ_STAGE_ROWS)
            + _row_chunks(wout_hbm.at[layer], wout_ref, PROJ_STAGE_ROWS), stage_proj, sem_proj)

    @pl.when(j == 0)
    def _():
        pool_c[...] = jnp.zeros_like(pool_c)
        k_c[...] = jnp.zeros_like(k_c)
        v_c[...] = jnp.zeros_like(v_c)

    def gate(i):
        lo = _C_GATE + i * D_MODEL
        return _sigmoid(_dot(h, w_in_ref[:, lo:lo + D_MODEL]))

    def cols_of(t):
        return [t[:, c * LANES:(c + 1) * LANES] for c in range(t.shape[1] // LANES)]

    def split_bf16(t):
        hi = t.astype(BF16)
        return hi, (t - hi.astype(F32)).astype(BF16)

    x = x_ref[0]
    h = _rms_rows(x, norm1_ref[...]).astype(BF16)
    z = _dot(h, w_in_ref[:, :_C_GATE])
    xp = z[:, _C_POOL:_C_POOL + POOL_WIDTH]
    q = z[:, _C_Q:_C_Q + ATTN_WIDTH]
    k = z[:, _C_K:_C_K + KV_WIDTH]
    v = z[:, _C_V:_C_V + KV_WIDTH]
    su = z[:, _C_SU:_C_SU + SGU_WIDTH]
    sv = z[:, _C_SV:_C_SV + SGU_WIDTH]
    g_a = gate(0)

    ext = jnp.concatenate([pool_c[...], xp], axis=0)
    pool_c[...] = xp[T - POOL_HALO:, :]
    e1 = ext + pltpu.roll(ext, 1, axis=0)
    e2 = e1 + pltpu.roll(e1, 2, axis=0)
    e3 = e2 + pltpu.roll(e2, 4, axis=0)
    e4 = e3 + pltpu.roll(e3, 8, axis=0)
    lane = lax.broadcasted_iota(jnp.int32, (T, POOL_WIDTH), 1)
    row = lax.broadcasted_iota(jnp.int32, (T, POOL_WIDTH), 0)
    g1, g2, g3 = lane < POOL_GROUP_DIM, lane < 2 * POOL_GROUP_DIM, lane < 3 * POOL_GROUP_DIM
    wsum = jnp.where(g1, e1[POOL_HALO:], jnp.where(g2, e2[POOL_HALO:],
                     jnp.where(g3, e3[POOL_HALO:], e4[POOL_HALO:])))
    win = jnp.where(g1, POOL_WINDOWS[0], jnp.where(g2, POOL_WINDOWS[1],
                    jnp.where(g3, POOL_WINDOWS[2], POOL_WINDOWS[3])))
    count = jnp.minimum(j * T + row + 1, win).astype(F32)
    diff = (wsum / count - xp).astype(BF16)
    u_act = jax.nn.gelu(su)
    v_act = jax.nn.gelu(sv)
    sq_parts = [split_bf16(t * t) for t in (q[:, :2 * LANES], q[:, 2 * LANES:], v_act)]
    kk_parts = split_bf16(k * k)

    seg = seg_ref[...]
    ms_q0, ms_q1, ms_v = [(_dot(hi, seg) + _dot(lo, seg)) * (1.0 / HEAD_DIM) for hi, lo in sq_parts]
    ms_k = (_dot(kk_parts[0], seg_ref[:LANES, :LANES])
            + _dot(kk_parts[1], seg_ref[:LANES, :LANES])) * (1.0 / HEAD_DIM)
    mixed = (_dot(diff, wpool_ref[...]) * pscale_ref[...]).astype(BF16)
    g_b = gate(1)

    if make_tables:
        cos_out, sin_out = outs_and_scratch[1:3]
        ang = rope_a[0].astype(F32) * rope_b[...]
        lane_r = lax.broadcasted_iota(jnp.int32, ang.shape, 1)
        sin, cos = _sincos(ang)
        sin = jnp.where((lane_r & (HEAD_DIM // 2)) == 0, -sin, sin)
        cos_out[0] = cos
        sin_out[0] = sin
    else:
        cos = rope_a[0]
        sin = rope_b[0]
    ms_q = jnp.concatenate([ms_q0, ms_q1], axis=1)
    q_n = q * lax.rsqrt(ms_q + EPS) * (qn_ref[...] * (HEAD_DIM ** -0.5))
    k_n = k * lax.rsqrt(ms_k + EPS) * kn_ref[...]
    q_cols = [tc * cos + _swap_halves(tc) * sin for tc in cols_of(q_n)]
    k_rot = k_n * cos + _swap_halves(k_n) * sin
    lo1 = lax.broadcasted_iota(jnp.int32, (T, LANES), 1) < HEAD_DIM
    q_lo = [jnp.where(lo1, qc, 0.0).astype(BF16) for qc in q_cols]
    q_hi = [jnp.where(lo1, 0.0, qc).astype(BF16) for qc in q_cols]
    k_dup = [t.astype(BF16) for t in _dup_heads(k_rot)]
    v_n = v_act * lax.rsqrt(ms_v + EPS) * vn_ref[...]
    v_lo = [jnp.where(lo1, vc, 0.0).astype(BF16) for vc in cols_of(v_n)]
    v_hi = [jnp.where(lo1, 0.0, vc).astype(BF16) for vc in cols_of(v_n)]
    wt = lax.broadcasted_iota(jnp.int32, ws_ref.shape, 0)
    wsrc = lax.broadcasted_iota(jnp.int32, ws_ref.shape, 1) & (CHUNK - 1)
    w_tril = jnp.where(wsrc <= wt, ws_ref[...], jnp.zeros_like(ws_ref))

    y_a = _dot(mixed, wpa_ref[...])
    scores = {}
    for n in range(n_blocks):
        r0, r1 = n * ATTN_BLOCK, (n + 1) * ATTN_BLOCK
        for hd in range(N_KV_HEADS):
            k_prev = k_c[hd] if n == 0 else k_dup[hd][r0 - ATTN_BLOCK:r0]
            k_cat = jnp.concatenate([k_prev, k_dup[hd][r0:r1]], axis=0)
            c0, c1 = 2 * hd, 2 * hd + 1
            q_all = jnp.concatenate([q_lo[c0][r0:r1], q_hi[c0][r0:r1],
                                     q_lo[c1][r0:r1], q_hi[c1][r0:r1]], axis=0)
            scores[n, hd] = lax.dot_general(k_cat, q_all, (((1,), (1,)), ((), ())),
                                            preferred_element_type=F32)
    g_c = gate(2)

    kj = lax.broadcasted_iota(jnp.int32, (2 * ATTN_BLOCK, ATTN_BLOCK), 0)
    qi = lax.broadcasted_iota(jnp.int32, (2 * ATTN_BLOCK, ATTN_BLOCK), 1)
    band = (kj > qi) & (kj <= qi + ATTN_BLOCK)
    first_band = band & (kj >= jnp.where(j == 0, ATTN_BLOCK, 0))
    probs, denoms = {}, {}
    for (n, hd), s in scores.items():
        mask = jnp.concatenate([first_band if n == 0 else band] * Q_PER_KV, axis=1)
        s = jnp.where(mask, s, -jnp.inf)
        sink = jnp.concatenate(
            [jnp.full((1, ATTN_BLOCK), sinks_ref[Q_PER_KV * hd + g], F32)
             for g in range(Q_PER_KV)], axis=1)
        m = jnp.maximum(jnp.max(s, axis=0, keepdims=True), sink)
        p = jnp.exp(s - m)
        denoms[n, hd] = jnp.sum(p, axis=0, keepdims=True) + jnp.exp(sink - m)
        probs[n, hd] = p.astype(BF16)

    v_t = v.T.astype(BF16)
    head_rows = [[None] * n_blocks for _ in range(N_Q_HEADS)]
    for n in range(n_blocks):
        r0, r1 = n * ATTN_BLOCK, (n + 1) * ATTN_BLOCK
        for hd in range(N_KV_HEADS):
            v_prev = v_c[...] if n == 0 else v_t[:, r0 - ATTN_BLOCK:r0]
            v_cat = jnp.concatenate([v_prev, v_t[:, r0:r1]], axis=1)
            o = _dot(v_cat, probs[n, hd])[hd * HEAD_DIM:(hd + 1) * HEAD_DIM] / denoms[n, hd]
            for g in range(Q_PER_KV):
                head_rows[Q_PER_KV * hd + g][n] = o[:, g * ATTN_BLOCK:(g + 1) * ATTN_BLOCK]
    attn_t = jnp.concatenate([jnp.concatenate(blocks, axis=1) for blocks in head_rows], axis=0)
    for hd in range(N_KV_HEADS):
        k_c[hd] = k_dup[hd][T - ATTN_BLOCK:]
    v_c[...] = v_t[:, T - ATTN_BLOCK:]
    gated = []
    for n in range(T // CHUNK):
        r0, r1 = n * CHUNK, (n + 1) * CHUNK
        s_cols = []
        for c in range(SGU_WIDTH // LANES):
            rhs = jnp.concatenate([v_lo[c][r0:r1], v_hi[c][r0:r1]], axis=0)
            s_cols.append(_dot(w_tril[:, 2 * c * CHUNK:2 * (c + 1) * CHUNK], rhs))
        gated.append(u_act[r0:r1] * (jnp.concatenate(s_cols, axis=1) + bs_ref[...]))

    attn = attn_t.T.astype(BF16)
    sgu = jnp.concatenate(gated, axis=0).astype(BF16)
    merged = g_a * y_a + g_b * _dot(attn, wpb_ref[...]) + g_c * _dot(sgu, wpc_ref[...])
    o_ref[0] = x + _dot(merged.astype(BF16), wout_ref[...])


def _mixer_layer(layer, x, rope_a, rope_b, sinks, norm1, w_in, qn, kn, seg, wpool, pscale, wpa, wpb,
                 wpc, vn, ws, bs, wout, *, make_tables):
    B, S, D = x.shape
    T = MIXER_TILE
    row_spec = lambda w: pl.BlockSpec((1, T, w), lambda b, j: (b, j, 0))
    hbm = pl.BlockSpec(memory_space=pl.ANY)
    vmem = lambda a: _const_spec(a.shape)
    table = jax.ShapeDtypeStruct((B, S, LANES), F32)
    out_x = jax.ShapeDtypeStruct(x.shape, x.dtype)
    return pl.pallas_call(
        functools.partial(_mixer_kernel, layer=layer, make_tables=make_tables),
        grid=(B, S // T),
        in_specs=[pl.BlockSpec(memory_space=pltpu.SMEM), row_spec(D),
                  row_spec(1) if make_tables else row_spec(LANES),
                  vmem(rope_b) if make_tables else row_spec(LANES),
                  vmem(norm1), hbm, vmem(qn), vmem(kn), vmem(seg), vmem(wpool),
                  vmem(pscale), hbm, hbm, hbm, vmem(vn), vmem(ws), vmem(bs), hbm],
        out_specs=[row_spec(D), row_spec(LANES), row_spec(LANES)] if make_tables else row_spec(D),
        out_shape=[out_x, table, table] if make_tables else out_x,
        scratch_shapes=[pltpu.VMEM((POOL_HALO, POOL_WIDTH), F32),
                        pltpu.VMEM((N_KV_HEADS, ATTN_BLOCK, LANES), BF16),
                        pltpu.VMEM((KV_WIDTH, ATTN_BLOCK), BF16),
                        pltpu.VMEM(w_in.shape[1:], BF16), pltpu.VMEM(wpa.shape[1:], BF16),
                        pltpu.VMEM(wpb.shape[1:], BF16), pltpu.VMEM(wpc.shape[1:], BF16),
                        pltpu.VMEM(wout.shape[1:], BF16),
                        pltpu.VMEM((WEIGHT_STAGE_SLOTS, W_IN_STAGE_ROWS, IN_COLS), F32),
                        pltpu.VMEM((WEIGHT_STAGE_SLOTS, PROJ_STAGE_ROWS, D_MODEL), F32),
                        pltpu.SemaphoreType.DMA((WEIGHT_STAGE_SLOTS,)),
                        pltpu.SemaphoreType.DMA((WEIGHT_STAGE_SLOTS,))],
        compiler_params=pltpu.CompilerParams(
            dimension_semantics=("arbitrary", "arbitrary"), vmem_limit_bytes=VMEM_LIMIT),
        name="mixer",
    )(sinks, x, rope_a, rope_b, norm1, w_in, qn, kn, seg, wpool, pscale, wpa, wpb, wpc, vn, ws, bs,
      wout)


def _ffn_kernel(*refs, layer):
    n_col = D_MODEL // LANES
    x_cols = refs[:n_col]
    (norm2_ref, wup_hbm, cw_ref, cb_ref, wdown_hbm, o_ref, tail, act_buf, stage,
     wup_ref, wdown_ref, stage_up, stage_down, sem_up, sem_down) = refs[n_col:]
    T = o_ref.shape[1]
    R = T // SUBLANES
    j = pl.program_id(1)

    @pl.when(_first_step())
    def _():
        _load_weights_bf16(_row_chunks(wup_hbm.at[layer], wup_ref, W_UP_STAGE_ROWS),
                           stage_up, sem_up)
        _load_weights_bf16(_row_chunks(wdown_hbm.at[layer], wdown_ref, PROJ_STAGE_ROWS),
                           stage_down, sem_down)

    @pl.when(j == 0)
    def _():
        tail[...] = jnp.zeros_like(tail)

    x = jnp.concatenate(
        [jnp.concatenate([xc[0, pl.ds(p, R, stride=SUBLANES), :] for xc in x_cols], axis=1)
         for p in range(SUBLANES)], axis=0)
    h = _rms_rows(x, norm2_ref[...]).astype(BF16)
    first_row = lax.broadcasted_iota(jnp.int32, (R, FFN_CHUNK), 0) == 0

    def up(c):
        lo = c * FFN_CHUNK
        return (_dot(h, wup_ref[:, lo:lo + FFN_CHUNK]),
                _dot(h, wup_ref[:, D_FF + lo:D_FF + lo + FFN_CHUNK]))

    def conv(u, lo, half, c):
        w = cw_ref[:, lo:lo + FFN_CHUNK]
        u6, u7 = u[6 * R:7 * R], u[7 * R:]
        s6 = jnp.where(first_row, tail[c, half, 0, SUBLANES - 1:, :], pltpu.roll(u6, 1, axis=0))
        s7 = jnp.where(first_row, tail[c, half, 1, SUBLANES - 1:, :], pltpu.roll(u7, 1, axis=0))
        tail[c, half, 0] = u6[R - SUBLANES:]
        tail[c, half, 1] = u7[R - SUBLANES:]
        prev1 = jnp.concatenate([s7, u[:7 * R]], axis=0)
        prev2 = jnp.concatenate([s6, s7, u[:6 * R]], axis=0)
        return u * w[2:3] + prev1 * w[1:2] + prev2 * w[0:1] + cb_ref[:, lo:lo + FFN_CHUNK]

    def act(c, u_gate, u_val):
        lo = c * FFN_CHUNK
        gate = conv(u_gate, lo, 0, c)
        a = gate * _sigmoid(gate) * conv(u_val, D_FF + lo, 1, c)
        act_buf[:, lo:lo + FFN_CHUNK] = a.astype(BF16)

    u_next = up(0)
    for c in range(FFN_CHUNKS):
        u_cur = u_next
        if c + 1 < FFN_CHUNKS:
            u_next = up(c + 1)
        act(c, *u_cur)
    out = x + _dot(act_buf[...], wdown_ref[...])
    for cb in range(n_col):
        for p in range(SUBLANES):
            stage[cb, pl.ds(p, R, stride=SUBLANES), :] = out[p * R:(p + 1) * R,
                                                             cb * LANES:(cb + 1) * LANES]
        o_ref[0, :, cb * LANES:(cb + 1) * LANES] = stage[cb]


def _ffn_layer(layer, x, norm2, wup, cw, cb, wdown):
    B, S, D = x.shape
    T = FFN_TILE
    row_spec = pl.BlockSpec((1, T, D), lambda b, j: (b, j, 0))
    hbm = pl.BlockSpec(memory_space=pl.ANY)
    return pl.pallas_call(
        functools.partial(_ffn_kernel, layer=layer),
        grid=(B, S // T),
        in_specs=[pl.BlockSpec((1, T, LANES), functools.partial(lambda b, j, c: (b, j, c), c=c))
                  for c in range(D // LANES)]
                 + [_const_spec(norm2.shape), hbm, _const_spec(cw.shape), _const_spec(cb.shape), hbm],
        out_specs=row_spec,
        out_shape=jax.ShapeDtypeStruct(x.shape, x.dtype),
        scratch_shapes=[pltpu.VMEM((FFN_CHUNKS, 2, CONV_WIDTH - 1, SUBLANES, FFN_CHUNK), F32),
                        pltpu.VMEM((T, D_FF), BF16),
                        pltpu.VMEM((D // LANES, T, LANES), F32),
                        pltpu.VMEM(wup.shape[1:], BF16), pltpu.VMEM(wdown.shape[1:], BF16),
                        pltpu.VMEM((WEIGHT_STAGE_SLOTS, W_UP_STAGE_ROWS, 2 * D_FF), F32),
                        pltpu.VMEM((WEIGHT_STAGE_SLOTS, PROJ_STAGE_ROWS, D_MODEL), F32),
                        pltpu.SemaphoreType.DMA((WEIGHT_STAGE_SLOTS,)),
                        pltpu.SemaphoreType.DMA((WEIGHT_STAGE_SLOTS,))],
        compiler_params=pltpu.CompilerParams(
            dimension_semantics=("arbitrary", "arbitrary"), vmem_limit_bytes=VMEM_LIMIT),
        name="ffn",
    )(*([x] * (D // LANES)), norm2, wup, cw, cb, wdown)


def _block_diag(blocks):
    G, a, b = blocks.shape
    eye = jnp.eye(G, dtype=blocks.dtype)
    return (eye[:, None, :, None] * blocks[:, :, None, :]).reshape(G * a, G * b)


def kernel(x, positions, norm1, w_in, q_norm, k_norm, sinks, w_pool, pool_scale, sgu_v_norm, w_s,
           b_s, w_proj_a, w_proj_b, w_proj_c, w_out, norm2, w_up, conv_w, conv_b, w_down):
    B, S = positions.shape
    inv_freq = ROPE_THETA ** (-jnp.arange(0, HEAD_DIM, 2, dtype=F32) / HEAD_DIM)
    freq = jnp.tile(inv_freq, LANES // (HEAD_DIM // 2))[None, :]
    rope = (positions.reshape(B, S, 1), freq)
    seg = _block_diag(jnp.ones((2 * LANES // HEAD_DIM, HEAD_DIM, HEAD_DIM), BF16))
    row = lambda a: a[None, :].astype(F32)
    for l in range(DEPTH):
        out = _mixer_layer(
            l, x, *rope, sinks[l], row(norm1[l]), w_in,
            row(jnp.tile(q_norm[l], N_Q_HEADS)), row(jnp.tile(k_norm[l], N_KV_HEADS)), seg,
            _block_diag(w_pool[l]).astype(BF16), row(pool_scale[l]),
            w_proj_a, w_proj_b, w_proj_c,
            row(jnp.tile(sgu_v_norm[l], SGU_WIDTH // POOL_GROUP_DIM)),
            jnp.transpose(w_s[l], (1, 0, 2)).reshape(CHUNK, -1).astype(BF16),
            jnp.repeat(b_s[l].T, POOL_GROUP_DIM, axis=1),
            w_out, make_tables=(l == 0))
        if l == 0:
            x, *rope = out
        else:
            x = out
        x = _ffn_layer(l, x, row(norm2[l]), w_up, conv_w[l], row(conv_b[l]), w_down)
    return x
```

```python
import functools
import math

import jax
import jax.numpy as jnp
from jax import lax
from jax.experimental import pallas as pl
from jax.experimental.pallas import tpu as pltpu

D_MODEL = 1024
DEPTH = 2
HEAD_DIM = 64
POOL_WINDOWS = (2, 4, 8, 16)
POOL_GROUP_DIM = 64
POOL_WIDTH = 256
N_Q_HEADS = 8
N_KV_HEADS = 2
Q_PER_KV = 4
ATTN_BLOCK = 128
ATTN_WIDTH = 512
KV_WIDTH = 128
CHUNK = 128
SGU_WIDTH = 256
N_BRANCHES = 3
D_FF = 2816
CONV_WIDTH = 3
ROPE_THETA = 10000.0
EPS = 1e-6

_C_POOL = 0
_C_Q = _C_POOL + POOL_WIDTH
_C_K = _C_Q + ATTN_WIDTH
_C_V = _C_K + KV_WIDTH
_C_SU = _C_V + KV_WIDTH
_C_SV = _C_SU + SGU_WIDTH
_C_GATE = _C_SV + SGU_WIDTH
IN_COLS = _C_GATE + N_BRANCHES * D_MODEL

LANES = 128
SUBLANES = 8
POOL_HALO = 16
MIXER_TILE = 512
FFN_TILE = 512
FFN_CHUNK = 256
FFN_CHUNKS = D_FF // FFN_CHUNK
W_IN_STAGE_ROWS = 64
W_UP_STAGE_ROWS = 64
PROJ_STAGE_ROWS = 256
WEIGHT_STAGE_SLOTS = 3
VMEM_LIMIT = 56 * 1024 * 1024

F32 = jnp.float32
BF16 = jnp.bfloat16


def _dot(a, b):
    return jnp.dot(a, b, preferred_element_type=F32)


def _const_spec(shape):
    zeros = (0,) * len(shape)
    return pl.BlockSpec(shape, lambda *_: zeros, pipeline_mode=pl.Buffered(1))


def _rms_rows(x, gain):
    ms = jnp.mean(x * x, axis=-1, keepdims=True)
    return x * lax.rsqrt(ms + EPS) * gain


def _load_weights_bf16(jobs, stage, sem):
    slots = stage.shape[0]

    def copy(i):
        rows, width = jobs[i][0].shape
        return pltpu.make_async_copy(jobs[i][0], stage.at[i % slots, :rows, :width],
                                     sem.at[i % slots])

    for i in range(min(slots - 1, len(jobs))):
        copy(i).start()
    for i, (src, dst) in enumerate(jobs):
        if i + slots - 1 < len(jobs):
            copy(i + slots - 1).start()
        copy(i).wait()
        rows, width = src.shape
        dst[...] = stage[i % slots, :rows, :width].astype(BF16)


def _row_chunks(src, dst, rows):
    total = dst.shape[0]
    assert total % rows == 0
    return [(src.at[pl.ds(r, rows), :], dst.at[pl.ds(r, rows), :]) for r in range(0, total, rows)]


def _first_step():
    return (pl.program_id(0) == 0) & (pl.program_id(1) == 0)


def _sigmoid(x):
    return 0.5 * jnp.tanh(0.5 * x) + 0.5


def _split_half_pi(pieces=4, bits=8):
    rem, parts = math.pi / 2, []
    for _ in range(pieces - 1):
        mant, exp = math.frexp(rem)
        part = math.floor(mant * 2 ** bits) / 2 ** bits * 2 ** exp
        parts.append(part)
        rem -= part
    return parts + [rem]


_HALF_PI_PARTS = _split_half_pi()
_SIN_COEF = (-1.6666654611e-1, 8.3321608736e-3, -1.9515295891e-4)
_COS_COEF = (4.166664568298827e-2, -1.388731625493765e-3, 2.443315711809948e-5)


def _sincos(x):
    n = jnp.floor(x * (2.0 / math.pi) + 0.5)
    r = x
    for part in _HALF_PI_PARTS:
        r = r - n * part
    r2 = r * r
    s = r + r * r2 * (_SIN_COEF[0] + r2 * (_SIN_COEF[1] + r2 * _SIN_COEF[2]))
    c = 1.0 - 0.5 * r2 + r2 * r2 * (_COS_COEF[0] + r2 * (_COS_COEF[1] + r2 * _COS_COEF[2]))
    quad = n.astype(jnp.int32)
    odd = (quad & 1) == 1
    sin_x = jnp.where(odd, c, s)
    cos_x = jnp.where(odd, s, c)
    sin_x = jnp.where((quad & 2) == 0, sin_x, -sin_x)
    cos_x = jnp.where(((quad + 1) & 2) == 0, cos_x, -cos_x)
    return sin_x, cos_x


def _swap_halves(x):
    lane = lax.broadcasted_iota(jnp.int32, x.shape, 1)
    fwd = pltpu.roll(x, LANES - HEAD_DIM // 2, axis=1)
    bwd = pltpu.roll(x, HEAD_DIM // 2, axis=1)
    return jnp.where((lane & (HEAD_DIM // 2)) == 0, fwd, bwd)


def _dup_heads(x):
    lane = lax.broadcasted_iota(jnp.int32, x.shape, 1)
    rolled = pltpu.roll(x, HEAD_DIM, axis=1)
    lo = lane < HEAD_DIM
    return jnp.where(lo, x, rolled), jnp.where(lo, rolled, x)


def _mixer_kernel(sinks_ref, x_ref, rope_a, rope_b, norm1_ref, w_in_hbm, qn_ref, kn_ref,
                  seg_ref, wpool_ref, pscale_ref, wpa_hbm, wpb_hbm, wpc_hbm, vn_ref, ws_ref,
                  bs_ref, wout_hbm, *outs_and_scratch, layer, make_tables):
    n_out = 3 if make_tables else 1
    o_ref = outs_and_scratch[0]
    (pool_c, k_c, v_c, w_in_ref, wpa_ref, wpb_ref, wpc_ref, wout_ref, stage_in, stage_proj,
     sem_in, sem_proj) = outs_and_scratch[n_out:]
    T = x_ref.shape[1]
    n_blocks = T // ATTN_BLOCK
    j = pl.program_id(1)

    @pl.when(_first_step())
    def _():
        _load_weights_bf16(_row_chunks(w_in_hbm.at[layer], w_in_ref, W_IN_STAGE_ROWS),
                           stage_in, sem_in)
        _load_weights_bf16(
            _row_chunks(wpa_hbm.at[layer], wpa_ref, PROJ_STAGE_ROWS)
            + _row_chunks(wpb_hbm.at[layer], wpb_ref, PROJ_STAGE_ROWS)
            + _row_chunks(wpc_hbm.at[layer], wpc_ref, PROJ_STAGE_ROWS)
            + _row_chunks(wout_hbm.at[layer], wout_ref, PROJ_STAGE_ROWS), stage_proj, sem_proj)

    @pl.when(j == 0)
    def _():
        pool_c[...] = jnp.zeros_like(pool_c)
        k_c[...] = jnp.zeros_like(k_c)
        v_c[...] = jnp.zeros_like(v_c)

    def gate(i):
        lo = _C_GATE + i * D_MODEL
        return _sigmoid(_dot(h, w_in_ref[:, lo:lo + D_MODEL]))

    def cols_of(t):
        return [t[:, c * LANES:(c + 1) * LANES] for c in range(t.shape[1] // LANES)]

    def split_bf16(t):
        hi = t.astype(BF16)
        return hi, (t - hi.astype(F32)).astype(BF16)

    x = x_ref[0]
    h = _rms_rows(x, norm1_ref[layer:layer + 1, :]).astype(BF16)
    z = _dot(h, w_in_ref[:, :_C_GATE])
    xp = z[:, _C_POOL:_C_POOL + POOL_WIDTH]
    q = z[:, _C_Q:_C_Q + ATTN_WIDTH]
    k = z[:, _C_K:_C_K + KV_WIDTH]
    v = z[:, _C_V:_C_V + KV_WIDTH]
    su = z[:, _C_SU:_C_SU + SGU_WIDTH]
    sv = z[:, _C_SV:_C_SV + SGU_WIDTH]
    g_a = gate(0)

    ext = jnp.concatenate([pool_c[...], xp], axis=0)
    pool_c[...] = xp[T - POOL_HALO:, :]
    e1 = ext + pltpu.roll(ext, 1, axis=0)
    e2 = e1 + pltpu.roll(e1, 2, axis=0)
    e3 = e2 + pltpu.roll(e2, 4, axis=0)
    e4 = e3 + pltpu.roll(e3, 8, axis=0)
    lane = lax.broadcasted_iota(jnp.int32, (T, POOL_WIDTH), 1)
    row = lax.broadcasted_iota(jnp.int32, (T, POOL_WIDTH), 0)
    g1, g2, g3 = lane < POOL_GROUP_DIM, lane < 2 * POOL_GROUP_DIM, lane < 3 * POOL_GROUP_DIM
    wsum = jnp.where(g1, e1[POOL_HALO:], jnp.where(g2, e2[POOL_HALO:],
                     jnp.where(g3, e3[POOL_HALO:], e4[POOL_HALO:])))
    win = jnp.where(g1, POOL_WINDOWS[0], jnp.where(g2, POOL_WINDOWS[1],
                    jnp.where(g3, POOL_WINDOWS[2], POOL_WINDOWS[3])))
    count = jnp.minimum(j * T + row + 1, win).astype(F32)
    diff = (wsum / count - xp).astype(BF16)
    u_act = jax.nn.gelu(su)
    v_act = jax.nn.gelu(sv)
    sq_parts = [split_bf16(t * t) for t in (q[:, :2 * LANES], q[:, 2 * LANES:], v_act)]
    kk_parts = split_bf16(k * k)

    seg = seg_ref[...]
    ms_q0, ms_q1, ms_v = [(_dot(hi, seg) + _dot(lo, seg)) * (1.0 / HEAD_DIM) for hi, lo in sq_parts]
    ms_k = (_dot(kk_parts[0], seg_ref[:LANES, :LANES])
            + _dot(kk_parts[1], seg_ref[:LANES, :LANES])) * (1.0 / HEAD_DIM)
    mixed = (_dot(diff, wpool_ref[layer]) * pscale_ref[layer:layer + 1, :]).astype(BF16)
    g_b = gate(1)

    if make_tables:
        cos_out, sin_out = outs_and_scratch[1:3]
        ang_t = rope_b[...] * rope_a[0, 0].astype(F32)
        lane_r = lax.broadcasted_iota(jnp.int32, ang_t.shape, 0)
        sin_t, cos_t = _sincos(ang_t)
        sin = jnp.where((lane_r & (HEAD_DIM // 2)) == 0, -sin_t, sin_t).T
        cos = cos_t.T
        cos_out[0] = cos
        sin_out[0] = sin
    else:
        cos = rope_a[0]
        sin = rope_b[0]
    ms_q = jnp.concatenate([ms_q0, ms_q1], axis=1)
    q_n = q * lax.rsqrt(ms_q + EPS) * (qn_ref[layer:layer + 1, :] * (HEAD_DIM ** -0.5))
    k_n = k * lax.rsqrt(ms_k + EPS) * kn_ref[layer:layer + 1, :]
    q_cols = [tc * cos + _swap_halves(tc) * sin for tc in cols_of(q_n)]
    k_rot = k_n * cos + _swap_halves(k_n) * sin
    lo1 = lax.broadcasted_iota(jnp.int32, (T, LANES), 1) < HEAD_DIM
    q_lo = [jnp.where(lo1, qc, 0.0).astype(BF16) for qc in q_cols]
    q_hi = [jnp.where(lo1, 0.0, qc).astype(BF16) for qc in q_cols]
    k_dup = [t.astype(BF16) for t in _dup_heads(k_rot)]
    v_n = v_act * lax.rsqrt(ms_v + EPS) * vn_ref[layer:layer + 1, :]
    v_lo = [jnp.where(lo1, vc, 0.0).astype(BF16) for vc in cols_of(v_n)]
    v_hi = [jnp.where(lo1, 0.0, vc).astype(BF16) for vc in cols_of(v_n)]
    ws = ws_ref[layer]
    wt = lax.broadcasted_iota(jnp.int32, ws.shape, 0)
    wsrc = lax.broadcasted_iota(jnp.int32, ws.shape, 1) & (CHUNK - 1)
    w_tril = jnp.where(wsrc <= wt, ws, jnp.zeros_like(ws))

    y_a = _dot(mixed, wpa_ref[...])
    scores = {}
    for n in range(n_blocks):
        r0, r1 = n * ATTN_BLOCK, (n + 1) * ATTN_BLOCK
        for hd in range(N_KV_HEADS):
            k_prev = k_c[hd] if n == 0 else k_dup[hd][r0 - ATTN_BLOCK:r0]
            k_cat = jnp.concatenate([k_prev, k_dup[hd][r0:r1]], axis=0)
            c0, c1 = 2 * hd, 2 * hd + 1
            q_all = jnp.concatenate([q_lo[c0][r0:r1], q_hi[c0][r0:r1],
                                     q_lo[c1][r0:r1], q_hi[c1][r0:r1]], axis=0)
            scores[n, hd] = lax.dot_general(k_cat, q_all, (((1,), (1,)), ((), ())),
                                            preferred_element_type=F32)
    g_c = gate(2)

    kj = lax.broadcasted_iota(jnp.int32, (2 * ATTN_BLOCK, ATTN_BLOCK), 0)
    qi = lax.broadcasted_iota(jnp.int32, (2 * ATTN_BLOCK, ATTN_BLOCK), 1)
    band = (kj > qi) & (kj <= qi + ATTN_BLOCK)
    first_band = band & (kj >= jnp.where(j == 0, ATTN_BLOCK, 0))
    probs, denoms = {}, {}
    for (n, hd), s in scores.items():
        mask = jnp.concatenate([first_band if n == 0 else band] * Q_PER_KV, axis=1)
        s = jnp.where(mask, s, -jnp.inf)
        sink = jnp.concatenate(
            [jnp.full((1, ATTN_BLOCK), sinks_ref[layer, Q_PER_KV * hd + g], F32)
             for g in range(Q_PER_KV)], axis=1)
        m = jnp.maximum(jnp.max(s, axis=0, keepdims=True), sink)
        p = jnp.exp(s - m)
        denoms[n, hd] = jnp.sum(p, axis=0, keepdims=True) + jnp.exp(sink - m)
        probs[n, hd] = p.astype(BF16)

    v_t = v.T.astype(BF16)
    head_rows = [[None] * n_blocks for _ in range(N_Q_HEADS)]
    for n in range(n_blocks):
        r0, r1 = n * ATTN_BLOCK, (n + 1) * ATTN_BLOCK
        for hd in range(N_KV_HEADS):
            v_prev = v_c[...] if n == 0 else v_t[:, r0 - ATTN_BLOCK:r0]
            v_cat = jnp.concatenate([v_prev, v_t[:, r0:r1]], axis=1)
            o = _dot(v_cat, probs[n, hd])[hd * HEAD_DIM:(hd + 1) * HEAD_DIM] / denoms[n, hd]
            for g in range(Q_PER_KV):
                head_rows[Q_PER_KV * hd + g][n] = o[:, g * ATTN_BLOCK:(g + 1) * ATTN_BLOCK]
    attn_t = jnp.concatenate([jnp.concatenate(blocks, axis=1) for blocks in head_rows], axis=0)
    for hd in range(N_KV_HEADS):
        k_c[hd] = k_dup[hd][T - ATTN_BLOCK:]
    v_c[...] = v_t[:, T - ATTN_BLOCK:]
    gated = []
    for n in range(T // CHUNK):
        r0, r1 = n * CHUNK, (n + 1) * CHUNK
        s_cols = []
        for c in range(SGU_WIDTH // LANES):
            rhs = jnp.concatenate([v_lo[c][r0:r1], v_hi[c][r0:r1]], axis=0)
            s_cols.append(_dot(w_tril[:, 2 * c * CHUNK:2 * (c + 1) * CHUNK], rhs))
        gated.append(u_act[r0:r1] * (jnp.concatenate(s_cols, axis=1) + bs_ref[layer]))

    attn = attn_t.T.astype(BF16)
    sgu = jnp.concatenate(gated, axis=0).astype(BF16)
    merged = g_a * y_a + g_b * _dot(attn, wpb_ref[...]) + g_c * _dot(sgu, wpc_ref[...])
    o_ref[0] = x + _dot(merged.astype(BF16), wout_ref[...])


def _mixer_layer(layer, x, rope_a, rope_b, sinks, norm1, w_in, qn, kn, seg, wpool, pscale, wpa, wpb,
                 wpc, vn, ws, bs, wout, *, make_tables):
    B, S, D = x.shape
    T = MIXER_TILE
    row_spec = lambda w: pl.BlockSpec((1, T, w), lambda b, j: (b, j, 0))
    hbm = pl.BlockSpec(memory_space=pl.ANY)
    vmem = lambda a: _const_spec(a.shape)
    table = jax.ShapeDtypeStruct((B, S, LANES), F32)
    out_x = jax.ShapeDtypeStruct(x.shape, x.dtype)
    return pl.pallas_call(
        functools.partial(_mixer_kernel, layer=layer, make_tables=make_tables),
        grid=(B, S // T),
        in_specs=[pl.BlockSpec(memory_space=pltpu.SMEM), row_spec(D),
                  pl.BlockSpec((1, 1, 1, T), lambda b, j: (b, j, 0, 0)) if make_tables
                  else row_spec(LANES),
                  vmem(rope_b) if make_tables else row_spec(LANES),
                  vmem(norm1), hbm, vmem(qn), vmem(kn), vmem(seg), vmem(wpool),
                  vmem(pscale), hbm, hbm, hbm, vmem(vn), vmem(ws), vmem(bs), hbm],
        out_specs=[row_spec(D), row_spec(LANES), row_spec(LANES)] if make_tables else row_spec(D),
        out_shape=[out_x, table, table] if make_tables else out_x,
        scratch_shapes=[pltpu.VMEM((POOL_HALO, POOL_WIDTH), F32),
                        pltpu.VMEM((N_KV_HEADS, ATTN_BLOCK, LANES), BF16),
                        pltpu.VMEM((KV_WIDTH, ATTN_BLOCK), BF16),
                        pltpu.VMEM(w_in.shape[1:], BF16), pltpu.VMEM(wpa.shape[1:], BF16),
                        pltpu.VMEM(wpb.shape[1:], BF16), pltpu.VMEM(wpc.shape[1:], BF16),
                        pltpu.VMEM(wout.shape[1:], BF16),
                        pltpu.VMEM((WEIGHT_STAGE_SLOTS, W_IN_STAGE_ROWS, IN_COLS), F32),
                        pltpu.VMEM((WEIGHT_STAGE_SLOTS, PROJ_STAGE_ROWS, D_MODEL), F32),
                        pltpu.SemaphoreType.DMA((WEIGHT_STAGE_SLOTS,)),
                        pltpu.SemaphoreType.DMA((WEIGHT_STAGE_SLOTS,))],
        compiler_params=pltpu.CompilerParams(
            dimension_semantics=("arbitrary", "arbitrary"), vmem_limit_bytes=VMEM_LIMIT),
        name="mixer",
    )(sinks, x, rope_a, rope_b, norm1, w_in, qn, kn, seg, wpool, pscale, wpa, wpb, wpc, vn, ws, bs,
      wout)


def _ffn_kernel(*refs, layer):
    n_col = D_MODEL // LANES
    x_cols = refs[:n_col]
    (norm2_ref, wup_hbm, cw_ref, cb_ref, wdown_hbm, o_ref, tail, act_buf, stage,
     wup_ref, wdown_ref, stage_up, stage_down, sem_up, sem_down) = refs[n_col:]
    T = o_ref.shape[1]
    R = T // SUBLANES
    j = pl.program_id(1)

    @pl.when(_first_step())
    def _():
        _load_weights_bf16(_row_chunks(wup_hbm.at[layer], wup_ref, W_UP_STAGE_ROWS),
                           stage_up, sem_up)
        _load_weights_bf16(_row_chunks(wdown_hbm.at[layer], wdown_ref, PROJ_STAGE_ROWS),
                           stage_down, sem_down)

    @pl.when(j == 0)
    def _():
        tail[...] = jnp.zeros_like(tail)

    x = jnp.concatenate(
        [jnp.concatenate([xc[0, pl.ds(p, R, stride=SUBLANES), :] for xc in x_cols], axis=1)
         for p in range(SUBLANES)], axis=0)
    h = _rms_rows(x, norm2_ref[layer:layer + 1, :]).astype(BF16)
    first_row = lax.broadcasted_iota(jnp.int32, (R, FFN_CHUNK), 0) == 0

    def up(c):
        lo = c * FFN_CHUNK
        return (_dot(h, wup_ref[:, lo:lo + FFN_CHUNK]),
                _dot(h, wup_ref[:, D_FF + lo:D_FF + lo + FFN_CHUNK]))

    def conv(u, lo, half, c):
        w = cw_ref[layer, :, lo:lo + FFN_CHUNK]
        u6, u7 = u[6 * R:7 * R], u[7 * R:]
        s6 = jnp.where(first_row, tail[c, half, 0, SUBLANES - 1:, :], pltpu.roll(u6, 1, axis=0))
        s7 = jnp.where(first_row, tail[c, half, 1, SUBLANES - 1:, :], pltpu.roll(u7, 1, axis=0))
        tail[c, half, 0] = u6[R - SUBLANES:]
        tail[c, half, 1] = u7[R - SUBLANES:]
        prev1 = jnp.concatenate([s7, u[:7 * R]], axis=0)
        prev2 = jnp.concatenate([s6, s7, u[:6 * R]], axis=0)
        return (u * w[2:3] + prev1 * w[1:2] + prev2 * w[0:1]
                + cb_ref[layer:layer + 1, lo:lo + FFN_CHUNK])

    def act(c, u_gate, u_val):
        lo = c * FFN_CHUNK
        gate = conv(u_gate, lo, 0, c)
        a = gate * _sigmoid(gate) * conv(u_val, D_FF + lo, 1, c)
        act_buf[:, lo:lo + FFN_CHUNK] = a.astype(BF16)

    u_next = up(0)
    for c in range(FFN_CHUNKS):
        u_cur = u_next
        if c + 1 < FFN_CHUNKS:
            u_next = up(c + 1)
        act(c, *u_cur)
    out = x + _dot(act_buf[...], wdown_ref[...])
    for cb in range(n_col):
        for p in range(SUBLANES):
            stage[cb, pl.ds(p, R, stride=SUBLANES), :] = out[p * R:(p + 1) * R,
                                                             cb * LANES:(cb + 1) * LANES]
        o_ref[0, :, cb * LANES:(cb + 1) * LANES] = stage[cb]


def _ffn_layer(layer, x, norm2, wup, cw, cb, wdown):
    B, S, D = x.shape
    T = FFN_TILE
    row_spec = pl.BlockSpec((1, T, D), lambda b, j: (b, j, 0))
    hbm = pl.BlockSpec(memory_space=pl.ANY)
    return pl.pallas_call(
        functools.partial(_ffn_kernel, layer=layer),
        grid=(B, S // T),
        in_specs=[pl.BlockSpec((1, T, LANES), functools.partial(lambda b, j, c: (b, j, c), c=c))
                  for c in range(D // LANES)]
                 + [_const_spec(norm2.shape), hbm, _const_spec(cw.shape), _const_spec(cb.shape), hbm],
        out_specs=row_spec,
        out_shape=jax.ShapeDtypeStruct(x.shape, x.dtype),
        scratch_shapes=[pltpu.VMEM((FFN_CHUNKS, 2, CONV_WIDTH - 1, SUBLANES, FFN_CHUNK), F32),
                        pltpu.VMEM((T, D_FF), BF16),
                        pltpu.VMEM((D // LANES, T, LANES), F32),
                        pltpu.VMEM(wup.shape[1:], BF16), pltpu.VMEM(wdown.shape[1:], BF16),
                        pltpu.VMEM((WEIGHT_STAGE_SLOTS, W_UP_STAGE_ROWS, 2 * D_FF), F32),
                        pltpu.VMEM((WEIGHT_STAGE_SLOTS, PROJ_STAGE_ROWS, D_MODEL), F32),
                        pltpu.SemaphoreType.DMA((WEIGHT_STAGE_SLOTS,)),
                        pltpu.SemaphoreType.DMA((WEIGHT_STAGE_SLOTS,))],
        compiler_params=pltpu.CompilerParams(
            dimension_semantics=("arbitrary", "arbitrary"), vmem_limit_bytes=VMEM_LIMIT),
        name="ffn",
    )(*([x] * (D // LANES)), norm2, wup, cw, cb, wdown)


def _block_diag(blocks):
    G, a, b = blocks.shape
    eye = jnp.eye(G, dtype=blocks.dtype)
    return (eye[:, None, :, None] * blocks[:, :, None, :]).reshape(G * a, G * b)


def kernel(x, positions, norm1, w_in, q_norm, k_norm, sinks, w_pool, pool_scale, sgu_v_norm, w_s,
           b_s, w_proj_a, w_proj_b, w_proj_c, w_out, norm2, w_up, conv_w, conv_b, w_down):
    B, S = positions.shape
    inv_freq = ROPE_THETA ** (-jnp.arange(0, HEAD_DIM, 2, dtype=F32) / HEAD_DIM)
    freq = jnp.tile(inv_freq, LANES // (HEAD_DIM // 2))[:, None]
    rope = (positions.reshape(B, S // MIXER_TILE, 1, MIXER_TILE), freq)
    seg = _block_diag(jnp.ones((2 * LANES // HEAD_DIM, HEAD_DIM, HEAD_DIM), BF16))
    qn = jnp.tile(q_norm, (1, N_Q_HEADS))
    kn = jnp.tile(k_norm, (1, N_KV_HEADS))
    vn = jnp.tile(sgu_v_norm, (1, SGU_WIDTH // POOL_GROUP_DIM))
    wpool = jax.vmap(_block_diag)(w_pool).astype(BF16)
    ws = jnp.transpose(w_s, (0, 2, 1, 3)).reshape(DEPTH, CHUNK, -1).astype(BF16)
    bs = jnp.repeat(jnp.swapaxes(b_s, 1, 2), POOL_GROUP_DIM, axis=2)
    for l in range(DEPTH):
        out = _mixer_layer(l, x, *rope, sinks, norm1, w_in, qn, kn, seg, wpool, pool_scale,
                           w_proj_a, w_proj_b, w_proj_c, vn, ws, bs, w_out, make_tables=(l == 0))
        if l == 0:
            x, *rope = out
        else:
            x = out
        x = _ffn_layer(l, x, norm2, w_up, conv_w, conv_b, w_down)
    return x
```
